```python
import jax, jax.numpy as jnp
from jax import lax
import numpy as np

D_MODEL = 4096
BATCH = 1
SEQ = 8192
DEPTH = 1

CHUNK = 64
N_META = 16
Q_BLOCK = 128
EPS = 1e-6

FOX_HEADS = 16
FOX_HEAD_DIM = 128
FOX_WIDTH = FOX_HEADS * FOX_HEAD_DIM

GLA_HEADS = 4
GLA_DK = 256
GLA_DV = 512
GLA_KW = GLA_HEADS * GLA_DK
GLA_VW = GLA_HEADS * GLA_DV
GLA_GATE_RANK = 16
GLA_GATE_TAU = 16.0

D_FF = 11008

IN_SIZES = (FOX_WIDTH, FOX_WIDTH, FOX_WIDTH, FOX_HEADS, GLA_KW, GLA_KW, GLA_VW, GLA_VW, GLA_GATE_RANK)
D_IN = FOX_WIDTH * 3 + FOX_HEADS + GLA_KW * 2 + GLA_VW * 2 + GLA_GATE_RANK

kernel_name = "fox_gla_gated_hybrid_macaron"


def _rmsnorm(x, g):
    x32 = x.astype(jnp.float32)
    r = x32 * lax.rsqrt(jnp.mean(x32 * x32, axis=-1, keepdims=True) + EPS)
    return (r * g.astype(jnp.float32)).astype(x.dtype)


def _swiglu(x, w1, w3, w2):
    return (jax.nn.silu(x @ w1) * (x @ w3)) @ w2


def _split_cols(p, sizes):
    idx = np.cumsum(np.array(sizes))[:-1].tolist()
    return jnp.split(p, idx, axis=-1)


def _fox_attention(q, k, v, f_logit):
    b, l, h, dh = q.shape
    scale = dh ** -0.5
    log_f = jax.nn.log_sigmoid(f_logit.astype(jnp.float32))
    c = jnp.cumsum(log_f, axis=1)
    lp = ((l + Q_BLOCK - 1) // Q_BLOCK) * Q_BLOCK
    nb = lp // Q_BLOCK
    pad = lp - l
    qp = jnp.pad(q, ((0, 0), (0, pad), (0, 0), (0, 0)))
    kp = jnp.pad(k, ((0, 0), (0, pad), (0, 0), (0, 0)))
    vp = jnp.pad(v, ((0, 0), (0, pad), (0, 0), (0, 0)))
    cp = jnp.pad(c, ((0, 0), (0, pad), (0, 0)))
    c_k = jnp.transpose(cp, (0, 2, 1))[:, :, None, :]
    k_pos = jnp.arange(lp, dtype=jnp.int32)
    q_blocks = jnp.transpose(qp.reshape(b, nb, Q_BLOCK, h, dh), (1, 0, 2, 3, 4))
    c_blocks = jnp.transpose(cp.reshape(b, nb, Q_BLOCK, h), (1, 0, 2, 3))
    starts = jnp.arange(nb, dtype=jnp.int32) * Q_BLOCK

    def one_block(args):
        qb, cb, start = args
        s = jnp.einsum('bqhd,bkhd->bhqk', qb, kp).astype(jnp.float32) * scale
        bias = jnp.transpose(cb, (0, 2, 1))[:, :, :, None] - c_k
        q_pos = start + jnp.arange(Q_BLOCK, dtype=jnp.int32)
        mask = k_pos[None, :] <= q_pos[:, None]
        s = jnp.where(mask[None, None], s + bias, -jnp.inf)
        p = jax.nn.softmax(s, axis=-1)
        return jnp.einsum('bhqk,bkhd->bqhd', p.astype(vp.dtype), vp)

    out = lax.map(one_block, (q_blocks, c_blocks, starts))
    out = jnp.transpose(out, (1, 0, 2, 3, 4)).reshape(b, lp, h, dh)
    return out[:, :l]


def _gla_chunk_causal(q, k, v, log_alpha):
    b, l, h, dk = q.shape
    dv = v.shape[-1]
    pad_left = (-N_META) % CHUNK
    pw = ((0, 0), (pad_left, 0), (0, 0), (0, 0))
    qp = jnp.pad(q, pw)
    kp = jnp.pad(k, pw)
    vp = jnp.pad(v, pw)
    lap = jnp.pad(log_alpha.astype(jnp.float32), pw)
    lt = l + pad_left
    nc = lt // CHUNK

    def to_chunks(t):
        return jnp.transpose(t.reshape(b, nc, CHUNK, h, t.shape[-1]), (1, 0, 2, 3, 4))

    qc, kc, vc, lac = to_chunks(qp), to_chunks(kp), to_chunks(vp), to_chunks(lap)
    cum = jnp.cumsum(lac, axis=2)
    total = cum[:, :, -1]
    k_dec = kc.astype(jnp.float32) * jnp.exp(total[:, :, None] - cum)

    def step(state, xs):
        q_i, k_i, v_i, tot_i = xs
        state = jnp.exp(tot_i)[..., None] * state + jnp.einsum(
            'bchk,bchv->bhkv', k_i, v_i.astype(jnp.float32))
        o_i = jnp.einsum('bchk,bhkv->bchv', q_i.astype(jnp.float32), state)
        return state, o_i

    s0 = jnp.zeros((b, h, dk, dv), jnp.float32)
    _, out = lax.scan(step, s0, (qc, k_dec, vc, total))
    out = jnp.transpose(out, (1, 0, 2, 3, 4)).reshape(b, lt, h, dv)
    return out[:, pad_left:].astype(v.dtype)


def setup_inputs(seed: int = 0) -> dict:
    key = jax.random.key(seed)
    ks = jax.random.split(key, 24)
    f32 = jnp.float32
    D = D_MODEL

    def nrm(k, shape, scale):
        return jax.random.normal(k, shape, f32) * scale

    def gain(k, shape):
        return 1.0 + 0.02 * jax.random.normal(k, shape, f32)

    return {
        "x": jax.random.normal(ks[0], (BATCH, SEQ, D), f32),
        "meta_tokens": nrm(ks[1], (N_META, D), 1.0),
        "norm1_g": gain(ks[2], (DEPTH, D)),
        "ffn1_w1": nrm(ks[3], (DEPTH, D, D_FF), D ** -0.5),
        "ffn1_w3": nrm(ks[4], (DEPTH, D, D_FF), D ** -0.5),
        "ffn1_w2": nrm(ks[5], (DEPTH, D_FF, D), D_FF ** -0.5),
        "norm_mix_g": gain(ks[6], (DEPTH, D)),
        "w_in": nrm(ks[7], (DEPTH, D, D_IN), D ** -0.5),
        "fox_f_bias": 2.0 + 0.1 * jax.random.normal(ks[8], (DEPTH, FOX_HEADS), f32),
        "gla_alpha_w2": nrm(ks[9], (DEPTH, GLA_GATE_RANK, GLA_KW), GLA_GATE_RANK ** -0.5),
        "gla_alpha_b": nrm(ks[10], (DEPTH, GLA_KW), 0.1),
        "gla_norm_g": gain(ks[11], (DEPTH, GLA_DV)),
        "w_gate": nrm(ks[12], (DEPTH, D, 2 * D), D ** -0.5),
        "w_proj_fox": nrm(ks[13], (DEPTH, FOX_WIDTH, D), FOX_WIDTH ** -0.5),
        "w_proj_gla": nrm(ks[14], (DEPTH, GLA_VW, D), GLA_VW ** -0.5),
        "w_out": nrm(ks[15], (DEPTH, D, D), D ** -0.5),
        "norm2_g": gain(ks[16], (DEPTH, D)),
        "ffn2_w1": nrm(ks[17], (DEPTH, D, D_FF), D ** -0.5),
        "ffn2_w3": nrm(ks[18], (DEPTH, D, D_FF), D ** -0.5),
        "ffn2_w2": nrm(ks[19], (DEPTH, D_FF, D), D_FF ** -0.5),
        "norm_final_g": gain(ks[20], (D,)),
    }


def reference(x, meta_tokens, norm1_g, ffn1_w1, ffn1_w3, ffn1_w2, norm_mix_g, w_in,
              fox_f_bias, gla_alpha_w2, gla_alpha_b, gla_norm_g, w_gate, w_proj_fox,
              w_proj_gla, w_out, norm2_g, ffn2_w1, ffn2_w3, ffn2_w2, norm_final_g):
    b = x.shape[0]
    meta = jnp.broadcast_to(meta_tokens[None].astype(x.dtype), (b, N_META, D_MODEL))
    h = jnp.concatenate([meta, x], axis=1)
    l = h.shape[1]

    for i in range(DEPTH):
        h = h + 0.5 * _swiglu(_rmsnorm(h, norm1_g[i]), ffn1_w1[i], ffn1_w3[i], ffn1_w2[i])

        n = _rmsnorm(h, norm_mix_g[i])
        p = n @ w_in[i]
        qa, ka, va, fa, qb, kb, vb, rb, ab = _split_cols(p, IN_SIZES)

        o_a = _fox_attention(
            qa.reshape(b, l, FOX_HEADS, FOX_HEAD_DIM),
            ka.reshape(b, l, FOX_HEADS, FOX_HEAD_DIM),
            va.reshape(b, l, FOX_HEADS, FOX_HEAD_DIM),
            fa + fox_f_bias[i])
        o_a = o_a.reshape(b, l, FOX_WIDTH)

        log_alpha = jax.nn.log_sigmoid(
            (ab @ gla_alpha_w2[i] + gla_alpha_b[i]).astype(jnp.float32)) / GLA_GATE_TAU
        o_b = _gla_chunk_causal(
            (qb * (GLA_DK ** -0.5)).reshape(b, l, GLA_HEADS, GLA_DK),
            kb.reshape(b, l, GLA_HEADS, GLA_DK),
            vb.reshape(b, l, GLA_HEADS, GLA_DV),
            log_alpha.reshape(b, l, GLA_HEADS, GLA_DK))
        o_b = _rmsnorm(o_b, gla_norm_g[i]).reshape(b, l, GLA_VW) * jax.nn.silu(rb)

        g_a, g_b = jnp.split(jax.nn.sigmoid(n @ w_gate[i]), 2, axis=-1)
        y = g_a * (o_a @ w_proj_fox[i]) + g_b * (o_b @ w_proj_gla[i])
        h = h + y @ w_out[i]

        h = h + 0.5 * _swiglu(_rmsnorm(h, norm2_g[i]), ffn2_w1[i], ffn2_w3[i], ffn2_w2[i])

    h = _rmsnorm(h, norm_final_g)
    return h[:, N_META:]
```

```python
import functools

import jax
import jax.numpy as jnp
from jax import lax
from jax.experimental import pallas as pl
from jax.experimental.pallas import tpu as pltpu

F32 = jnp.float32
BF16 = jnp.bfloat16

EPS = 1e-6
N_META = 16
CHUNK = 64
FOX_HEADS = 16
FOX_HEAD_DIM = 128
FOX_WIDTH = FOX_HEADS * FOX_HEAD_DIM
GLA_HEADS = 4
GLA_DK = 256
GLA_DV = 512
GLA_KW = GLA_HEADS * GLA_DK
GLA_VW = GLA_HEADS * GLA_DV
GLA_GATE_RANK = 16
GLA_GATE_TAU = 16.0

LANE = 128
META_BLOCK = LANE
NEG_BIG = -1e30
VMEM_LIMIT = 56 * 1024 * 1024


def _log_sigmoid(z):
    return jnp.minimum(z, 0.0) - jnp.log(1.0 + jnp.exp(-jnp.abs(z)))


def _sigmoid(z):
    return 1.0 / (1.0 + jnp.exp(-z))


def _rmsnorm_body(x_ref, g_ref, o_ref):
    x = x_ref[...]
    ms = jnp.mean(x * x, axis=-1, keepdims=True)
    o_ref[...] = (x * lax.rsqrt(ms + EPS) * g_ref[...]).astype(o_ref.dtype)


def _rmsnorm(x, g, rows, tile, out_dtype):
    d = x.shape[1]
    return pl.pallas_call(
        _rmsnorm_body,
        grid=(rows // tile,),
        in_specs=[pl.BlockSpec((tile, d), lambda i: (i, 0)),
                  pl.BlockSpec((1, d), lambda i: (0, 0))],
        out_specs=pl.BlockSpec((tile, d), lambda i: (i, 0)),
        out_shape=jax.ShapeDtypeStruct((rows, d), out_dtype),
        compiler_params=pltpu.CompilerParams(
            dimension_semantics=("parallel",), vmem_limit_bytes=VMEM_LIMIT),
        name="rmsnorm",
    )(x, g.reshape(1, d))


def _mm_body(n_a, pair_meta, n_extra, epilogue, tk, nk_total, *refs):
    n_p = len(pair_meta)
    a_refs = refs[:n_a]
    b_refs = refs[n_a:n_a + n_p]
    e_refs = refs[n_a + n_p:n_a + n_p + n_extra]
    o_ref = refs[n_a + n_p + n_extra]
    acc_refs = refs[n_a + n_p + n_extra + 1:]
    k = pl.program_id(2)

    for p, (ai, nk_p, k_valid) in enumerate(pair_meta):
        acc = acc_refs[p]

        @pl.when(k == 0)
        def _():
            acc[...] = jnp.zeros_like(acc)

        def _accumulate(acc=acc, ai=ai, p=p, k_valid=k_valid):
            b = b_refs[p][...]
            if k_valid is not None:
                rows = lax.broadcasted_iota(jnp.int32, b.shape, 0) + k * tk
                b = jnp.where(rows < k_valid, b, jnp.zeros_like(b))
            acc[...] += jnp.dot(a_refs[ai][...], b.astype(BF16),
                                preferred_element_type=F32)

        if nk_p == nk_total:
            _accumulate()
        else:
            pl.when(k < nk_p)(_accumulate)

    @pl.when(k == nk_total - 1)
    def _():
        accs = [r[...] for r in acc_refs]
        extras = [r[...] for r in e_refs]
        o_ref[...] = epilogue(accs, extras, pl.program_id(1)).astype(o_ref.dtype)


def _fused_matmul(a_list, pairs, extras, epilogue, *, m_rows, n_cols, out_dtype,
                  tm, tn, tk, name):
    nks = [pl.cdiv(a_list[ai].shape[1], tk) for ai, _, _ in pairs]
    nk_total = max(nks)
    a_nk = {}
    for (ai, _, _), nk_p in zip(pairs, nks):
        a_nk[ai] = nk_p
    pair_meta = []
    in_specs = []
    for ai, a in enumerate(a_list):
        nk_a = a_nk[ai]
        in_specs.append(pl.BlockSpec(
            (tm, tk), lambda i, j, k, nk_a=nk_a: (i, jnp.minimum(k, nk_a - 1))))
    for (ai, b, col_off), nk_p in zip(pairs, nks):
        k_rows = b.shape[0]
        k_valid = k_rows if k_rows % tk else None
        pair_meta.append((ai, nk_p, k_valid))
        in_specs.append(pl.BlockSpec(
            (tk, tn),
            lambda i, j, k, nk_p=nk_p, col_off=col_off:
                (jnp.minimum(k, nk_p - 1), j + col_off)))
    for _ in extras:
        in_specs.append(pl.BlockSpec((tm, tn), lambda i, j, k: (i, j)))
    body = functools.partial(_mm_body, len(a_list), tuple(pair_meta), len(extras),
                             epilogue, tk, nk_total)
    return pl.pallas_call(
        body,
        grid=(m_rows // tm, pl.cdiv(n_cols, tn), nk_total),
        in_specs=in_specs,
        out_specs=pl.BlockSpec((tm, tn), lambda i, j, k: (i, j)),
        out_shape=jax.ShapeDtypeStruct((m_rows, n_cols), out_dtype),
        scratch_shapes=[pltpu.VMEM((tm, tn), F32) for _ in pairs],
        compiler_params=pltpu.CompilerParams(
            dimension_semantics=("parallel", "parallel", "arbitrary"),
            vmem_limit_bytes=VMEM_LIMIT),
        name=name,
    )(*a_list, *[b for _, b, _ in pairs], *extras)


def _swiglu_ffn(n, h_res, w1, w3, w2, *, m_rows, tm):
    d_ff = w1.shape[1]
    d = w2.shape[1]
    tn_up, tk_up = 512, 1024
    tk_down = 512
    f_pad = pl.cdiv(d_ff, tn_up) * tn_up

    def up_epilogue(accs, extras, j):
        a1, a3 = accs
        u = a1 * _sigmoid(a1) * a3
        cols = lax.broadcasted_iota(jnp.int32, u.shape, 1) + j * tn_up
        return jnp.where(cols < d_ff, u, 0.0)

    u = _fused_matmul([n], [(0, w1, 0), (0, w3, 0)], [], up_epilogue,
                      m_rows=m_rows, n_cols=f_pad, out_dtype=BF16,
                      tm=tm, tn=tn_up, tk=tk_up, name="ffn_up")

    def down_epilogue(accs, extras, j):
        return extras[0] + 0.5 * accs[0]

    return _fused_matmul([u], [(0, w2, 0)], [h_res], down_epilogue,
                         m_rows=m_rows, n_cols=d, out_dtype=F32,
                         tm=tm, tn=512, tk=tk_down, name="ffn_down")


def _small_proj_body(n_ref, wft_ref, fb_ref, wab_ref, kbias_ref, ab_ref, carry_ref):
    t = pl.program_id(0)
    is_meta = t == 0

    @pl.when(is_meta)
    def _():
        carry_ref[...] = jnp.zeros_like(carry_ref)

    n = n_ref[...]
    ft = lax.dot_general(wft_ref[...].astype(BF16), n, (((1,), (1,)), ((), ())),
                         preferred_element_type=F32)
    logf = _log_sigmoid(ft + fb_ref[...])
    lane = lax.broadcasted_iota(jnp.int32, logf.shape, 1)
    valid = jnp.logical_or(jnp.logical_not(is_meta), lane < N_META)
    logf = jnp.where(valid, logf, 0.0)
    tsz = logf.shape[1]
    upper = (lax.broadcasted_iota(jnp.int32, (tsz, tsz), 0)
             <= lax.broadcasted_iota(jnp.int32, (tsz, tsz), 1)).astype(F32)
    c = carry_ref[...] + jnp.dot(logf, upper, precision=lax.Precision.HIGHEST,
                                 preferred_element_type=F32)
    kbias_ref[...] = jnp.where(valid, -c, NEG_BIG)
    carry_ref[...] = carry_ref[...] + jnp.sum(logf, axis=1, keepdims=True)
    ab_ref[...] = jnp.dot(n, wab_ref[...].astype(BF16), preferred_element_type=F32)


def _small_proj(n, wf_t, f_bias, w_ab, *, seq):
    lp, d = n.shape
    heads = wf_t.shape[0]
    rank = w_ab.shape[1]
    tsz = META_BLOCK
    n_x = seq // tsz

    def row_block(t):
        return jnp.where(t == 0, n_x, t - 1)

    return pl.pallas_call(
        _small_proj_body,
        grid=(n_x + 1,),
        in_specs=[pl.BlockSpec((tsz, d), lambda t: (row_block(t), 0)),
                  pl.BlockSpec((heads, d), lambda t: (0, 0)),
                  pl.BlockSpec((heads, LANE), lambda t: (0, 0)),
                  pl.BlockSpec((d, rank), lambda t: (0, 0))],
        out_specs=[pl.BlockSpec((heads, tsz), lambda t: (0, row_block(t))),
                   pl.BlockSpec((tsz, rank), lambda t: (row_block(t), 0))],
        out_shape=[jax.ShapeDtypeStruct((heads, lp), F32),
                   jax.ShapeDtypeStruct((lp, rank), F32)],
        scratch_shapes=[pltpu.VMEM((heads, LANE), F32)],
        compiler_params=pltpu.CompilerParams(
            dimension_semantics=("arbitrary",), vmem_limit_bytes=VMEM_LIMIT),
        name="small_proj",
    )(n, wf_t, jnp.broadcast_to(f_bias.reshape(heads, 1), (heads, LANE)), w_ab)


def _fox_body(tq, tk, q_ref, km_ref, vm_ref, bm_ref, kx_ref, vx_ref, bx_ref,
              o_ref, m_ref, l_ref, acc_ref):
    i = pl.program_id(1)
    j = pl.program_id(2)
    nj = pl.num_programs(2)

    def online_update(k, v, bias, causal):
        q = q_ref[...]
        s = lax.dot_general(q, k, (((1,), (1,)), ((), ())),
                            preferred_element_type=F32) + bias
        if causal:
            row = lax.broadcasted_iota(jnp.int32, s.shape, 0)
            col = lax.broadcasted_iota(jnp.int32, s.shape, 1)
            s = jnp.where(col <= row, s, NEG_BIG)
        m_prev = m_ref[...]
        m_next = jnp.maximum(m_prev, jnp.max(s, axis=1, keepdims=True))
        p = jnp.exp(s - jnp.tile(m_next, (1, s.shape[1] // LANE)))
        alpha = jnp.exp(m_prev - m_next)
        l_ref[...] = alpha * l_ref[...] + jnp.sum(p, axis=1, keepdims=True)
        m_ref[...] = m_next
        acc_ref[...] = alpha * acc_ref[...] + jnp.dot(
            p.astype(BF16), v, preferred_element_type=F32)

    @pl.when(j == 0)
    def _():
        m_ref[...] = jnp.full_like(m_ref, NEG_BIG)
        l_ref[...] = jnp.zeros_like(l_ref)
        acc_ref[...] = jnp.zeros_like(acc_ref)
        online_update(km_ref[...], vm_ref[...], bm_ref[...], False)

    @pl.when(jnp.logical_and(j >= 1, j - 1 < i))
    def _():
        online_update(kx_ref[...], vx_ref[...], bx_ref[...], False)

    @pl.when(j - 1 == i)
    def _():
        online_update(kx_ref[...], vx_ref[...], bx_ref[...], True)

    @pl.when(j == nj - 1)
    def _():
        o_ref[...] = (acc_ref[...] / l_ref[...]).astype(o_ref.dtype)


def _fox_attention(qkv, kbias, *, seq, tq, tk):
    assert tq == tk
    heads = kbias.shape[0]
    dh = FOX_HEAD_DIM
    n_q = seq // tq
    meta_blk = seq // META_BLOCK

    def kv_block(i, j):
        return jnp.clip(j - 1, 0, i)

    return pl.pallas_call(
        functools.partial(_fox_body, tq, tk),
        grid=(heads, n_q, n_q + 1),
        in_specs=[
            pl.BlockSpec((tq, dh), lambda h, i, j: (i, h)),
            pl.BlockSpec((META_BLOCK, dh), lambda h, i, j: (meta_blk, heads + h)),
            pl.BlockSpec((META_BLOCK, dh), lambda h, i, j: (meta_blk, 2 * heads + h)),
            pl.BlockSpec((None, 1, META_BLOCK), lambda h, i, j: (h, 0, meta_blk)),
            pl.BlockSpec((tk, dh), lambda h, i, j: (kv_block(i, j), heads + h)),
            pl.BlockSpec((tk, dh), lambda h, i, j: (kv_block(i, j), 2 * heads + h)),
            pl.BlockSpec((None, 1, tk), lambda h, i, j: (h, 0, kv_block(i, j))),
        ],
        out_specs=pl.BlockSpec((tq, dh), lambda h, i, j: (i, h)),
        out_shape=jax.ShapeDtypeStruct((seq, heads * dh), BF16),
        scratch_shapes=[pltpu.VMEM((tq, LANE), F32), pltpu.VMEM((tq, LANE), F32),
                        pltpu.VMEM((tq, dh), F32)],
        compiler_params=pltpu.CompilerParams(
            dimension_semantics=("parallel", "parallel", "arbitrary"),
            vmem_limit_bytes=VMEM_LIMIT),
        name="fox_attention",
    )(qkv, qkv, qkv, kbias, qkv, qkv, kbias)


def _gla_body(qv_ref, kr_ref, ab_ref, w2_ref, b_ref, g_ref, o_ref, state_ref):
    t = pl.program_id(0)
    is_meta = t == 0

    @pl.when(is_meta)
    def _():
        state_ref[...] = jnp.zeros_like(state_ref)

    z = jnp.dot(ab_ref[...].astype(BF16), w2_ref[...].astype(BF16),
                preferred_element_type=F32) + b_ref[...]
    la = _log_sigmoid(z) / GLA_GATE_TAU
    row = lax.broadcasted_iota(jnp.int32, la.shape, 0)
    la = jnp.where(jnp.logical_and(is_meta, row >= N_META), 0.0, la)
    c = la.shape[0]
    lower = (lax.broadcasted_iota(jnp.int32, (c, c), 1)
             <= lax.broadcasted_iota(jnp.int32, (c, c), 0)).astype(F32)
    cum = jnp.dot(lower, la, precision=lax.Precision.HIGHEST,
                  preferred_element_type=F32)
    total = cum[c - 1:c, :]
    k_dec = (kr_ref[:, :GLA_KW] * jnp.exp(total - cum)).astype(BF16)
    decay = jnp.exp(total)

    for h in range(GLA_HEADS):
        ks = slice(h * GLA_DK, (h + 1) * GLA_DK)
        vs = slice(GLA_KW + h * GLA_DV, GLA_KW + (h + 1) * GLA_DV)
        kv = lax.dot_general(qv_ref[:, vs], k_dec[:, ks], (((0,), (0,)), ((), ())),
                             preferred_element_type=F32)
        s_new = state_ref[h] * decay[:, ks] + kv
        state_ref[h] = s_new
        o = lax.dot_general(qv_ref[:, ks], s_new.astype(BF16),
                            (((1,), (1,)), ((), ())),
                            preferred_element_type=F32)
        ms = jnp.mean(o * o, axis=-1, keepdims=True)
        r = kr_ref[:, vs]
        o_ref[:, h * GLA_DV:(h + 1) * GLA_DV] = (
            o * lax.rsqrt(ms + EPS) * g_ref[...] * (r * _sigmoid(r))
        ).astype(o_ref.dtype)


def _gla(qv, kr, ab, w2, b, g, *, seq):
    n_x = seq // CHUNK

    def row_block(t):
        return jnp.where(t == 0, n_x, t - 1)

    width = GLA_KW + GLA_VW
    return pl.pallas_call(
        _gla_body,
        grid=(n_x + 1,),
        in_specs=[pl.BlockSpec((CHUNK, width), lambda t: (row_block(t), 0)),
                  pl.BlockSpec((CHUNK, width), lambda t: (row_block(t), 0)),
                  pl.BlockSpec((CHUNK, GLA_GATE_RANK), lambda t: (row_block(t), 0)),
                  pl.BlockSpec((GLA_GATE_RANK, GLA_KW), lambda t: (0, 0)),
                  pl.BlockSpec((1, GLA_KW), lambda t: (0, 0)),
                  pl.BlockSpec((1, GLA_DV), lambda t: (0, 0))],
        out_specs=pl.BlockSpec((CHUNK, GLA_VW), lambda t: (jnp.maximum(t - 1, 0), 0)),
        out_shape=jax.ShapeDtypeStruct((seq, GLA_VW), BF16),
        scratch_shapes=[pltpu.VMEM((GLA_HEADS, GLA_DV, GLA_DK), F32)],
        compiler_params=pltpu.CompilerParams(
            dimension_semantics=("arbitrary",), vmem_limit_bytes=VMEM_LIMIT),
        name="gla",
    )(qv, kr, ab, w2, b.reshape(1, GLA_KW), g.reshape(1, GLA_DV))


def kernel(x, meta_tokens, norm1_g, ffn1_w1, ffn1_w3, ffn1_w2, norm_mix_g, w_in,
           fox_f_bias, gla_alpha_w2, gla_alpha_b, gla_norm_g, w_gate, w_proj_fox,
           w_proj_gla, w_out, norm2_g, ffn2_w1, ffn2_w3, ffn2_w2, norm_final_g):
    batch, seq, d = x.shape
    assert batch == 1 and seq % 1024 == 0
    depth = norm1_g.shape[0]
    lp = seq + META_BLOCK
    assert lp % 4 == 0 and (lp // 4) % 16 == 0
    tm_all = lp // 4
    tm_x = seq // 4

    pad = jnp.zeros((META_BLOCK - N_META, d), x.dtype)
    h = jnp.concatenate([x[0], meta_tokens.astype(x.dtype), pad], axis=0)

    o_fa = FOX_WIDTH * 3
    o_qb = o_fa + FOX_HEADS
    o_kb = o_qb + GLA_KW
    o_vb = o_kb + GLA_KW
    o_rb = o_vb + GLA_VW
    o_ab = o_rb + GLA_VW
    fox_scale = FOX_HEAD_DIM ** -0.5
    gla_scale = GLA_DK ** -0.5

    assert depth == 1
    for li in range(depth):
        n1 = _rmsnorm(h, norm1_g[li], lp, lp // 20, BF16)
        h = _swiglu_ffn(n1, h, ffn1_w1[li], ffn1_w3[li], ffn1_w2[li], m_rows=lp,
                        tm=tm_all)

        n = _rmsnorm(h, norm_mix_g[li], lp, lp // 20, BF16)
        w = w_in[li]

        tn_p = 512

        def fox_epilogue(accs, extras, j):
            scale = jnp.where(j < FOX_WIDTH // tn_p, fox_scale, 1.0)
            return accs[0] * scale

        qkv_a = _fused_matmul([n], [(0, w, 0)], [], fox_epilogue, m_rows=lp,
                              n_cols=3 * FOX_WIDTH, out_dtype=BF16,
                              tm=tm_all, tn=tn_p, tk=1024, name="proj_fox")

        w_qv = jnp.concatenate([w[:, o_qb:o_kb], w[:, o_vb:o_rb]], axis=1)
        w_kr = jnp.concatenate([w[:, o_kb:o_vb], w[:, o_rb:o_ab]], axis=1)

        def gla_q_epilogue(accs, extras, j):
            scale = jnp.where(j < GLA_KW // tn_p, gla_scale, 1.0)
            return accs[0] * scale

        qv_b = _fused_matmul([n], [(0, w_qv, 0)], [], gla_q_epilogue, m_rows=lp,
                             n_cols=GLA_KW + GLA_VW, out_dtype=BF16,
                             tm=tm_all, tn=tn_p, tk=1024, name="proj_gla_qv")
        kr_b = _fused_matmul([n], [(0, w_kr, 0)], [], lambda a, e, j: a[0], m_rows=lp,
                             n_cols=GLA_KW + GLA_VW, out_dtype=F32,
                             tm=tm_all, tn=tn_p, tk=1024, name="proj_gla_kr")

        kbias, ab = _small_proj(n, w[:, o_fa:o_qb].T, fox_f_bias[li], w[:, o_ab:],
                                seq=seq)
        o_a = _fox_attention(qkv_a, kbias.reshape(FOX_HEADS, 1, lp), seq=seq,
                             tq=1024, tk=1024)
        o_b = _gla(qv_b, kr_b, ab, gla_alpha_w2[li], gla_alpha_b[li], gla_norm_g[li],
                   seq=seq)

        wg = w_gate[li]

        def mix_epilogue(accs, extras, j):
            ga, gb, pa, pb = accs
            return _sigmoid(ga) * pa + _sigmoid(gb) * pb

        tn_m = 512
        y = _fused_matmul(
            [n, o_a, o_b],
            [(0, wg, 0), (0, wg, d // tn_m), (1, w_proj_fox[li], 0),
             (2, w_proj_gla[li], 0)],
            [], mix_epilogue, m_rows=seq, n_cols=d, out_dtype=BF16,
            tm=1024, tn=tn_m, tk=1024, name="mix")

        h = _fused_matmul([y], [(0, w_out[li], 0)], [h],
                          lambda a, e, j: e[0] + a[0], m_rows=seq, n_cols=d,
                          out_dtype=F32, tm=tm_x, tn=512, tk=1024, name="out_proj")

        n2 = _rmsnorm(h, norm2_g[li], seq, 512, BF16)
        h = _swiglu_ffn(n2, h, ffn2_w1[li], ffn2_w3[li], ffn2_w2[li], m_rows=seq,
                        tm=tm_x)

    out = _rmsnorm(h, norm_final_g, seq, 512, F32)
    return out.reshape(batch, seq, d)
```

```python
import functools

import jax
import jax.numpy as jnp
from jax import lax
from jax.experimental import pallas as pl
from jax.experimental.pallas import tpu as pltpu

F32 = jnp.float32
BF16 = jnp.bfloat16

EPS = 1e-6
N_META = 16
CHUNK = 64
FOX_HEADS = 16
FOX_HEAD_DIM = 128
FOX_WIDTH = FOX_HEADS * FOX_HEAD_DIM
GLA_HEADS = 4
GLA_DK = 256
GLA_DV = 512
GLA_KW = GLA_HEADS * GLA_DK
GLA_VW = GLA_HEADS * GLA_DV
GLA_GATE_RANK = 16
GLA_GATE_TAU = 16.0

LANE = 128
META_BLOCK = LANE
NEG_BIG = -1e30
VMEM_LIMIT = 56 * 1024 * 1024


def _log_sigmoid(z):
    return jnp.minimum(z, 0.0) - jnp.log(1.0 + jnp.exp(-jnp.abs(z)))


def _sigmoid(z):
    return 1.0 / (1.0 + jnp.exp(-z))


def _rmsnorm_body(x_ref, g_ref, o_ref):
    x = x_ref[...]
    ms = jnp.mean(x * x, axis=-1, keepdims=True)
    o_ref[...] = (x * lax.rsqrt(ms + EPS) * g_ref[...]).astype(o_ref.dtype)


def _rmsnorm(x, g, rows, tile, out_dtype):
    d = x.shape[1]
    return pl.pallas_call(
        _rmsnorm_body,
        grid=(rows // tile,),
        in_specs=[pl.BlockSpec((tile, d), lambda i: (i, 0)),
                  pl.BlockSpec((1, d), lambda i: (0, 0))],
        out_specs=pl.BlockSpec((tile, d), lambda i: (i, 0)),
        out_shape=jax.ShapeDtypeStruct((rows, d), out_dtype),
        compiler_params=pltpu.CompilerParams(
            dimension_semantics=("parallel",), vmem_limit_bytes=VMEM_LIMIT),
        name="rmsnorm",
    )(x, g.reshape(1, d))


def _mm_fullk_body(pair_ai, n_a, n_extra, epilogue, *refs):
    n_p = len(pair_ai)
    a_refs = refs[:n_a]
    b_refs = refs[n_a:n_a + n_p]
    e_refs = refs[n_a + n_p:n_a + n_p + n_extra]
    o_ref = refs[n_a + n_p + n_extra]
    dots = [jnp.dot(a_refs[ai][...], b_refs[p][...].astype(BF16),
                    preferred_element_type=F32) for p, ai in enumerate(pair_ai)]
    extras = [r[...] for r in e_refs]
    o_ref[...] = epilogue(dots, extras, pl.program_id(1)).astype(o_ref.dtype)


def _matmul_fullk(a_list, pairs, extras, epilogue, *, m_rows, n_cols, out_dtype,
                  tm, tn, name, a_buffers=2):
    assert m_rows % tm == 0 and n_cols % tn == 0
    a_mode = {} if a_buffers == 2 else {"pipeline_mode": pl.Buffered(a_buffers)}
    in_specs = [pl.BlockSpec((tm, a.shape[1]), lambda i, j: (i, 0), **a_mode)
                for a in a_list]
    for ai, b, col_off in pairs:
        assert b.shape[0] == a_list[ai].shape[1]
        in_specs.append(pl.BlockSpec((b.shape[0], tn),
                                     lambda i, j, col_off=col_off: (0, j + col_off)))
    for _ in extras:
        in_specs.append(pl.BlockSpec((tm, tn), lambda i, j: (i, j)))
    body = functools.partial(_mm_fullk_body, tuple(ai for ai, _, _ in pairs),
                             len(a_list), len(extras), epilogue)
    return pl.pallas_call(
        body,
        grid=(m_rows // tm, n_cols // tn),
        in_specs=in_specs,
        out_specs=pl.BlockSpec((tm, tn), lambda i, j: (i, j)),
        out_shape=jax.ShapeDtypeStruct((m_rows, n_cols), out_dtype),
        compiler_params=pltpu.CompilerParams(
            dimension_semantics=("parallel", "arbitrary"),
            vmem_limit_bytes=VMEM_LIMIT),
        name=name,
    )(*a_list, *[b for _, b, _ in pairs], *extras)


def _mm_kacc_body(tk, k_total, scale, a_ref, b_ref, r_ref, o_ref):
    k = pl.program_id(2)

    def partial_product():
        a = a_ref[...]
        b = b_ref[...]
        if k_total % tk:
            limit = k_total - k * tk
            a = jnp.where(lax.broadcasted_iota(jnp.int32, a.shape, 1) < limit, a,
                          jnp.zeros_like(a))
            b = jnp.where(lax.broadcasted_iota(jnp.int32, b.shape, 0) < limit, b,
                          jnp.zeros_like(b))
        return scale * jnp.dot(a, b.astype(BF16), preferred_element_type=F32)

    @pl.when(k == 0)
    def _():
        o_ref[...] = r_ref[...] + partial_product()

    @pl.when(k > 0)
    def _():
        o_ref[...] += partial_product()


def _matmul_kacc(a, b, res, scale, *, m_rows, tm, tn, tk, name):
    k_total, n_cols = b.shape
    assert a.shape[1] == k_total and m_rows % tm == 0 and n_cols % tn == 0
    return pl.pallas_call(
        functools.partial(_mm_kacc_body, tk, k_total, scale),
        grid=(m_rows // tm, n_cols // tn, pl.cdiv(k_total, tk)),
        in_specs=[pl.BlockSpec((tm, tk), lambda i, j, k: (i, k)),
                  pl.BlockSpec((tk, tn), lambda i, j, k: (k, j)),
                  pl.BlockSpec((tm, tn), lambda i, j, k: (i, j))],
        out_specs=pl.BlockSpec((tm, tn), lambda i, j, k: (i, j)),
        out_shape=jax.ShapeDtypeStruct((m_rows, n_cols), F32),
        compiler_params=pltpu.CompilerParams(
            dimension_semantics=("parallel", "parallel", "arbitrary"),
            vmem_limit_bytes=VMEM_LIMIT),
        name=name,
    )(a, b, res)


def _swiglu_ffn(n, h_res, w1, w3, w2, *, m_rows, tm_up, tm_down):
    def up_epilogue(dots, extras, j):
        a1, a3 = dots
        return a1 * _sigmoid(a1) * a3

    u = _matmul_fullk([n], [(0, w1, 0), (0, w3, 0)], [], up_epilogue,
                      m_rows=m_rows, n_cols=w1.shape[1], out_dtype=BF16,
                      tm=tm_up, tn=256, name="ffn_up")
    return _matmul_kacc(u, w2, h_res, 0.5, m_rows=m_rows, tm=tm_down, tn=1024, tk=512,
                        name="ffn_down")


def _small_proj_body(n_ref, wft_ref, fb_ref, wab_ref, kbias_ref, ab_ref, carry_ref):
    t = pl.program_id(0)
    is_meta = t == 0

    @pl.when(is_meta)
    def _():
        carry_ref[...] = jnp.zeros_like(carry_ref)

    n = n_ref[...]
    ft = lax.dot_general(wft_ref[...].astype(BF16), n, (((1,), (1,)), ((), ())),
                         preferred_element_type=F32)
    logf = _log_sigmoid(ft + fb_ref[...])
    lane = lax.broadcasted_iota(jnp.int32, logf.shape, 1)
    valid = jnp.logical_or(jnp.logical_not(is_meta), lane < N_META)
    logf = jnp.where(valid, logf, 0.0)
    tsz = logf.shape[1]
    upper = (lax.broadcasted_iota(jnp.int32, (tsz, tsz), 0)
             <= lax.broadcasted_iota(jnp.int32, (tsz, tsz), 1)).astype(F32)
    c = carry_ref[...] + jnp.dot(logf, upper, precision=lax.Precision.HIGHEST,
                                 preferred_element_type=F32)
    kbias_ref[...] = jnp.where(valid, -c, NEG_BIG)
    carry_ref[...] = carry_ref[...] + jnp.sum(logf, axis=1, keepdims=True)
    ab_ref[...] = jnp.dot(n, wab_ref[...].astype(BF16), preferred_element_type=F32)


def _small_proj(n, wf_t, f_bias, w_ab, *, seq):
    lp, d = n.shape
    heads = wf_t.shape[0]
    rank = w_ab.shape[1]
    tsz = META_BLOCK
    n_x = seq // tsz

    def row_block(t):
        return jnp.where(t == 0, n_x, t - 1)

    return pl.pallas_call(
        _small_proj_body,
        grid=(n_x + 1,),
        in_specs=[pl.BlockSpec((tsz, d), lambda t: (row_block(t), 0)),
                  pl.BlockSpec((heads, d), lambda t: (0, 0)),
                  pl.BlockSpec((heads, LANE), lambda t: (0, 0)),
                  pl.BlockSpec((d, rank), lambda t: (0, 0))],
        out_specs=[pl.BlockSpec((heads, tsz), lambda t: (0, row_block(t))),
                   pl.BlockSpec((tsz, rank), lambda t: (row_block(t), 0))],
        out_shape=[jax.ShapeDtypeStruct((heads, lp), F32),
                   jax.ShapeDtypeStruct((lp, rank), F32)],
        scratch_shapes=[pltpu.VMEM((heads, LANE), F32)],
        compiler_params=pltpu.CompilerParams(
            dimension_semantics=("arbitrary",), vmem_limit_bytes=VMEM_LIMIT),
        name="small_proj",
    )(n, wf_t, jnp.broadcast_to(f_bias.reshape(heads, 1), (heads, LANE)), w_ab)


def _fox_body(tq, tk, q_ref, km_ref, vm_ref, bm_ref, kx_ref, vx_ref, bx_ref,
              o_ref, m_ref, l_ref, acc_ref):
    i = pl.program_id(1)
    j = pl.program_id(2)
    nj = pl.num_programs(2)

    def online_update(k, v, bias, causal):
        q = q_ref[...]
        s = lax.dot_general(q, k, (((1,), (1,)), ((), ())),
                            preferred_element_type=F32) + bias
        if causal:
            row = lax.broadcasted_iota(jnp.int32, s.shape, 0)
            col = lax.broadcasted_iota(jnp.int32, s.shape, 1)
            s = jnp.where(col <= row, s, NEG_BIG)
        m_prev = m_ref[...]
        m_next = jnp.maximum(m_prev, jnp.max(s, axis=1, keepdims=True))
        p = jnp.exp(s - jnp.tile(m_next, (1, s.shape[1] // LANE)))
        alpha = jnp.exp(m_prev - m_next)
        l_ref[...] = alpha * l_ref[...] + jnp.sum(p, axis=1, keepdims=True)
        m_ref[...] = m_next
        acc_ref[...] = alpha * acc_ref[...] + jnp.dot(
            p.astype(BF16), v, preferred_element_type=F32)

    @pl.when(j == 0)
    def _():
        m_ref[...] = jnp.full_like(m_ref, NEG_BIG)
        l_ref[...] = jnp.zeros_like(l_ref)
        acc_ref[...] = jnp.zeros_like(acc_ref)
        online_update(km_ref[...], vm_ref[...], bm_ref[...], False)

    @pl.when(jnp.logical_and(j >= 1, j - 1 < i))
    def _():
        online_update(kx_ref[...], vx_ref[...], bx_ref[...], False)

    @pl.when(j - 1 == i)
    def _():
        online_update(kx_ref[...], vx_ref[...], bx_ref[...], True)

    @pl.when(j == nj - 1)
    def _():
        o_ref[...] = (acc_ref[...] / l_ref[...]).astype(o_ref.dtype)


def _fox_attention(qkv, kbias, *, seq, tq, tk):
    assert tq == tk
    heads = kbias.shape[0]
    dh = FOX_HEAD_DIM
    n_q = seq // tq
    meta_blk = seq // META_BLOCK

    def kv_block(i, j):
        return jnp.clip(j - 1, 0, i)

    return pl.pallas_call(
        functools.partial(_fox_body, tq, tk),
        grid=(heads, n_q, n_q + 1),
        in_specs=[
            pl.BlockSpec((tq, dh), lambda h, i, j: (i, h)),
            pl.BlockSpec((META_BLOCK, dh), lambda h, i, j: (meta_blk, heads + h)),
            pl.BlockSpec((META_BLOCK, dh), lambda h, i, j: (meta_blk, 2 * heads + h)),
            pl.BlockSpec((None, 1, META_BLOCK), lambda h, i, j: (h, 0, meta_blk)),
            pl.BlockSpec((tk, dh), lambda h, i, j: (kv_block(i, j), heads + h)),
            pl.BlockSpec((tk, dh), lambda h, i, j: (kv_block(i, j), 2 * heads + h)),
            pl.BlockSpec((None, 1, tk), lambda h, i, j: (h, 0, kv_block(i, j))),
        ],
        out_specs=pl.BlockSpec((tq, dh), lambda h, i, j: (i, h)),
        out_shape=jax.ShapeDtypeStruct((seq, heads * dh), BF16),
        scratch_shapes=[pltpu.VMEM((tq, LANE), F32), pltpu.VMEM((tq, LANE), F32),
                        pltpu.VMEM((tq, dh), F32)],
        compiler_params=pltpu.CompilerParams(
            dimension_semantics=("parallel", "parallel", "arbitrary"),
            vmem_limit_bytes=VMEM_LIMIT),
        name="fox_attention",
    )(qkv, qkv, qkv, kbias, qkv, qkv, kbias)


def _gla_body(qv_ref, kr_ref, ab_ref, w2_ref, b_ref, g_ref, o_ref, state_ref):
    t = pl.program_id(0)
    is_meta = t == 0

    @pl.when(is_meta)
    def _():
        state_ref[...] = jnp.zeros_like(state_ref)

    z = jnp.dot(ab_ref[...].astype(BF16), w2_ref[...].astype(BF16),
                preferred_element_type=F32) + b_ref[...]
    la = _log_sigmoid(z) / GLA_GATE_TAU
    row = lax.broadcasted_iota(jnp.int32, la.shape, 0)
    la = jnp.where(jnp.logical_and(is_meta, row >= N_META), 0.0, la)
    c = la.shape[0]
    lower = (lax.broadcasted_iota(jnp.int32, (c, c), 1)
             <= lax.broadcasted_iota(jnp.int32, (c, c), 0)).astype(F32)
    cum = jnp.dot(lower, la, precision=lax.Precision.HIGHEST,
                  preferred_element_type=F32)
    total = cum[c - 1:c, :]
    k_dec = (kr_ref[:, :GLA_KW] * jnp.exp(total - cum)).astype(BF16)
    decay = jnp.exp(total)

    for h in range(GLA_HEADS):
        ks = slice(h * GLA_DK, (h + 1) * GLA_DK)
        vs = slice(GLA_KW + h * GLA_DV, GLA_KW + (h + 1) * GLA_DV)
        kv = lax.dot_general(qv_ref[:, vs], k_dec[:, ks], (((0,), (0,)), ((), ())),
                             preferred_element_type=F32)
        s_new = state_ref[h] * decay[:, ks] + kv
        state_ref[h] = s_new
        o = lax.dot_general(qv_ref[:, ks], s_new.astype(BF16),
                            (((1,), (1,)), ((), ())),
                            preferred_element_type=F32)
        ms = jnp.mean(o * o, axis=-1, keepdims=True)
        r = kr_ref[:, vs]
        o_ref[:, h * GLA_DV:(h + 1) * GLA_DV] = (
            o * lax.rsqrt(ms + EPS) * g_ref[...] * (r * _sigmoid(r))
        ).astype(o_ref.dtype)


def _gla(qv, kr, ab, w2, b, g, *, seq):
    n_x = seq // CHUNK

    def row_block(t):
        return jnp.where(t == 0, n_x, t - 1)

    width = GLA_KW + GLA_VW
    return pl.pallas_call(
        _gla_body,
        grid=(n_x + 1,),
        in_specs=[pl.BlockSpec((CHUNK, width), lambda t: (row_block(t), 0)),
                  pl.BlockSpec((CHUNK, width), lambda t: (row_block(t), 0)),
                  pl.BlockSpec((CHUNK, GLA_GATE_RANK), lambda t: (row_block(t), 0)),
                  pl.BlockSpec((GLA_GATE_RANK, GLA_KW), lambda t: (0, 0)),
                  pl.BlockSpec((1, GLA_KW), lambda t: (0, 0)),
                  pl.BlockSpec((1, GLA_DV), lambda t: (0, 0))],
        out_specs=pl.BlockSpec((CHUNK, GLA_VW), lambda t: (jnp.maximum(t - 1, 0), 0)),
        out_shape=jax.ShapeDtypeStruct((seq, GLA_VW), BF16),
        scratch_shapes=[pltpu.VMEM((GLA_HEADS, GLA_DV, GLA_DK), F32)],
        compiler_params=pltpu.CompilerParams(
            dimension_semantics=("arbitrary",), vmem_limit_bytes=VMEM_LIMIT),
        name="gla",
    )(qv, kr, ab, w2, b.reshape(1, GLA_KW), g.reshape(1, GLA_DV))


def kernel(x, meta_tokens, norm1_g, ffn1_w1, ffn1_w3, ffn1_w2, norm_mix_g, w_in,
           fox_f_bias, gla_alpha_w2, gla_alpha_b, gla_norm_g, w_gate, w_proj_fox,
           w_proj_gla, w_out, norm2_g, ffn2_w1, ffn2_w3, ffn2_w2, norm_final_g):
    batch, seq, d = x.shape
    assert batch == 1 and seq % 1024 == 0
    depth = norm1_g.shape[0]
    lp = seq + META_BLOCK
    assert lp % 8 == 0 and (lp // 8) % 16 == 0
    tm_all = lp // 8
    tm_x = seq // 8

    pad = jnp.zeros((META_BLOCK - N_META, d), x.dtype)
    h = jnp.concatenate([x[0], meta_tokens.astype(x.dtype), pad], axis=0)

    o_fa = FOX_WIDTH * 3
    o_qb = o_fa + FOX_HEADS
    o_kb = o_qb + GLA_KW
    o_vb = o_kb + GLA_KW
    o_rb = o_vb + GLA_VW
    o_ab = o_rb + GLA_VW
    fox_scale = FOX_HEAD_DIM ** -0.5
    gla_scale = GLA_DK ** -0.5

    assert depth == 1
    for li in range(depth):
        n1 = _rmsnorm(h, norm1_g[li], lp, lp // 20, BF16)
        h = _swiglu_ffn(n1, h, ffn1_w1[li], ffn1_w3[li], ffn1_w2[li], m_rows=lp,
                        tm_up=tm_all, tm_down=2 * tm_all)

        n = _rmsnorm(h, norm_mix_g[li], lp, lp // 20, BF16)
        w = w_in[li]

        tn_p = 512

        def fox_epilogue(dots, extras, j):
            scale = jnp.where(j < FOX_WIDTH // tn_p, fox_scale, 1.0)
            return dots[0] * scale

        qkv_a = _matmul_fullk([n], [(0, w, 0)], [], fox_epilogue, m_rows=lp,
                              n_cols=3 * FOX_WIDTH, out_dtype=BF16,
                              tm=tm_all, tn=tn_p, name="proj_fox")

        w_qv = jnp.concatenate([w[:, o_qb:o_kb], w[:, o_vb:o_rb]], axis=1)
        w_kr = jnp.concatenate([w[:, o_kb:o_vb], w[:, o_rb:o_ab]], axis=1)

        def gla_q_epilogue(dots, extras, j):
            scale = jnp.where(j < GLA_KW // tn_p, gla_scale, 1.0)
            return dots[0] * scale

        qv_b = _matmul_fullk([n], [(0, w_qv, 0)], [], gla_q_epilogue, m_rows=lp,
                             n_cols=GLA_KW + GLA_VW, out_dtype=BF16,
                             tm=tm_all, tn=tn_p, name="proj_gla_qv")
        kr_b = _matmul_fullk([n], [(0, w_kr, 0)], [], lambda a, e, j: a[0], m_rows=lp,
                             n_cols=GLA_KW + GLA_VW, out_dtype=F32,
                             tm=tm_all, tn=tn_p, name="proj_gla_kr")

        kbias, ab = _small_proj(n, w[:, o_fa:o_qb].T, fox_f_bias[li], w[:, o_ab:],
                                seq=seq)
        o_a = _fox_attention(qkv_a, kbias.reshape(FOX_HEADS, 1, lp), seq=seq,
                             tq=1024, tk=1024)
        o_b = _gla(qv_b, kr_b, ab, gla_alpha_w2[li], gla_alpha_b[li], gla_norm_g[li],
                   seq=seq)

        wg = w_gate[li]

        def mix_epilogue(accs, extras, j):
            ga, gb, pa, pb = accs
            return _sigmoid(ga) * pa + _sigmoid(gb) * pb

        tn_m = 256
        y = _matmul_fullk(
            [n, o_a, o_b],
            [(0, wg, 0), (0, wg, d // tn_m), (1, w_proj_fox[li], 0),
             (2, w_proj_gla[li], 0)],
            [], mix_epilogue, m_rows=seq, n_cols=d, out_dtype=BF16,
            tm=tm_x, tn=tn_m, name="mix", a_buffers=1)

        h = _matmul_fullk([y], [(0, w_out[li], 0)], [h],
                          lambda a, e, j: e[0] + a[0], m_rows=seq, n_cols=d,
                          out_dtype=F32, tm=tm_x, tn=512, name="out_proj")

        n2 = _rmsnorm(h, norm2_g[li], seq, 512, BF16)
        h = _swiglu_ffn(n2, h, ffn2_w1[li], ffn2_w3[li], ffn2_w2[li], m_rows=seq,
                        tm_up=tm_x, tm_down=2 * tm_x)

    out = _rmsnorm(h, norm_final_g, seq, 512, F32)
    return out.reshape(batch, seq, d)
```

```python
import functools

import jax
import jax.numpy as jnp
from jax import lax
from jax.experimental import pallas as pl
from jax.experimental.pallas import tpu as pltpu

F32 = jnp.float32
BF16 = jnp.bfloat16

EPS = 1e-6
N_META = 16
CHUNK = 64
FOX_HEADS = 16
FOX_HEAD_DIM = 128
FOX_WIDTH = FOX_HEADS * FOX_HEAD_DIM
GLA_HEADS = 4
GLA_DK = 256
GLA_DV = 512
GLA_KW = GLA_HEADS * GLA_DK
GLA_VW = GLA_HEADS * GLA_DV
GLA_GATE_RANK = 16
GLA_GATE_TAU = 16.0

LANE = 128
META_BLOCK = LANE
NEG_BIG = -1e30
LOG2E = 1.4426950408889634
VMEM_LIMIT = 56 * 1024 * 1024


def _log_sigmoid(z):
    return jnp.minimum(z, 0.0) - jnp.log(1.0 + jnp.exp(-jnp.abs(z)))


def _sigmoid(z):
    return 1.0 / (1.0 + jnp.exp(-z))


def _rmsnorm_body(x_ref, g_ref, o_ref):
    x = x_ref[...]
    ms = jnp.mean(x * x, axis=-1, keepdims=True)
    o_ref[...] = (x * lax.rsqrt(ms + EPS) * g_ref[...]).astype(o_ref.dtype)


def _rmsnorm(x, g, rows, out_dtype):
    d = x.shape[1]
    tile = max(t for t in range(16, 513, 16) if rows % t == 0)
    return pl.pallas_call(
        _rmsnorm_body,
        grid=(rows // tile,),
        in_specs=[pl.BlockSpec((tile, d), lambda i: (i, 0)),
                  pl.BlockSpec((1, d), lambda i: (0, 0))],
        out_specs=pl.BlockSpec((tile, d), lambda i: (i, 0)),
        out_shape=jax.ShapeDtypeStruct((rows, d), out_dtype),
        compiler_params=pltpu.CompilerParams(
            dimension_semantics=("parallel",), vmem_limit_bytes=VMEM_LIMIT),
        name="rmsnorm",
    )(x, g.reshape(1, d))


def _mm_fullk_body(pair_ai, n_a, n_extra, epilogue, *refs):
    n_p = len(pair_ai)
    a_refs = refs[:n_a]
    b_refs = refs[n_a:n_a + n_p]
    e_refs = refs[n_a + n_p:n_a + n_p + n_extra]
    o_ref = refs[n_a + n_p + n_extra]
    dots = [jnp.dot(a_refs[ai][...], b_refs[p][...].astype(BF16),
                    preferred_element_type=F32) for p, ai in enumerate(pair_ai)]
    extras = [r[...] for r in e_refs]
    o_ref[...] = epilogue(dots, extras, pl.program_id(1)).astype(o_ref.dtype)


def _matmul_fullk(a_list, pairs, extras, epilogue, *, m_rows, n_cols, out_dtype,
                  tm, tn, name, a_buffers=2):
    assert m_rows % tm == 0 and n_cols % tn == 0
    a_mode = {} if a_buffers == 2 else {"pipeline_mode": pl.Buffered(a_buffers)}
    in_specs = [pl.BlockSpec((tm, a.shape[1]), lambda i, j: (i, 0), **a_mode)
                for a in a_list]
    for ai, b, col_off in pairs:
        assert b.shape[0] == a_list[ai].shape[1]
        in_specs.append(pl.BlockSpec((b.shape[0], tn),
                                     lambda i, j, col_off=col_off: (0, j + col_off)))
    for _ in extras:
        in_specs.append(pl.BlockSpec((tm, tn), lambda i, j: (i, j)))
    body = functools.partial(_mm_fullk_body, tuple(ai for ai, _, _ in pairs),
                             len(a_list), len(extras), epilogue)
    return pl.pallas_call(
        body,
        grid=(m_rows // tm, n_cols // tn),
        in_specs=in_specs,
        out_specs=pl.BlockSpec((tm, tn), lambda i, j: (i, j)),
        out_shape=jax.ShapeDtypeStruct((m_rows, n_cols), out_dtype),
        compiler_params=pltpu.CompilerParams(
            dimension_semantics=("parallel", "arbitrary"),
            vmem_limit_bytes=VMEM_LIMIT),
        name=name,
    )(*a_list, *[b for _, b, _ in pairs], *extras)


def _mm_kacc_body(tk, k_total, scale, a_ref, b_ref, r_ref, o_ref):
    k = pl.program_id(2)

    def partial_product():
        a = a_ref[...]
        b = b_ref[...]
        if k_total % tk:
            limit = k_total - k * tk
            a = jnp.where(lax.broadcasted_iota(jnp.int32, a.shape, 1) < limit, a,
                          jnp.zeros_like(a))
            b = jnp.where(lax.broadcasted_iota(jnp.int32, b.shape, 0) < limit, b,
                          jnp.zeros_like(b))
        return scale * jnp.dot(a, b.astype(BF16), preferred_element_type=F32)

    @pl.when(k == 0)
    def _():
        o_ref[...] = r_ref[...] + partial_product()

    @pl.when(k > 0)
    def _():
        o_ref[...] += partial_product()


def _matmul_kacc(a, b, res, scale, *, m_rows, tm, tn, tk, name):
    k_total, n_cols = b.shape
    assert a.shape[1] == k_total and m_rows % tm == 0 and n_cols % tn == 0
    return pl.pallas_call(
        functools.partial(_mm_kacc_body, tk, k_total, scale),
        grid=(m_rows // tm, n_cols // tn, pl.cdiv(k_total, tk)),
        in_specs=[pl.BlockSpec((tm, tk), lambda i, j, k: (i, k)),
                  pl.BlockSpec((tk, tn), lambda i, j, k: (k, j)),
                  pl.BlockSpec((tm, tn), lambda i, j, k: (i, j))],
        out_specs=pl.BlockSpec((tm, tn), lambda i, j, k: (i, j)),
        out_shape=jax.ShapeDtypeStruct((m_rows, n_cols), F32),
        compiler_params=pltpu.CompilerParams(
            dimension_semantics=("parallel", "parallel", "arbitrary"),
            vmem_limit_bytes=VMEM_LIMIT),
        name=name,
    )(a, b, res)


def _swiglu_ffn(n, h_res, w1, w3, w2, *, m_rows, tm_up, tm_down):
    def up_epilogue(dots, extras, j):
        a1, a3 = dots
        return a1 * _sigmoid(a1) * a3

    u = _matmul_fullk([n], [(0, w1, 0), (0, w3, 0)], [], up_epilogue,
                      m_rows=m_rows, n_cols=w1.shape[1], out_dtype=BF16,
                      tm=tm_up, tn=256, name="ffn_up")
    return _matmul_kacc(u, w2, h_res, 0.5, m_rows=m_rows, tm=tm_down, tn=1024, tk=512,
                        name="ffn_down")


def _small_proj_body(n_ref, wf_ref, fb_ref, wab_ref, kaug_ref, ab_ref, carry_ref):
    t = pl.program_id(0)
    is_meta = t == 0

    @pl.when(is_meta)
    def _():
        carry_ref[...] = jnp.zeros_like(carry_ref)

    n = n_ref[...]
    ft = jnp.dot(n, wf_ref[...].astype(BF16), preferred_element_type=F32)
    logf = _log_sigmoid(ft + fb_ref[...])
    row = lax.broadcasted_iota(jnp.int32, logf.shape, 0)
    col = lax.broadcasted_iota(jnp.int32, logf.shape, 1)
    valid = jnp.logical_or(jnp.logical_not(is_meta), row < N_META)
    logf = jnp.where(valid, logf, 0.0)
    tsz = logf.shape[0]
    lower = (lax.broadcasted_iota(jnp.int32, (tsz, tsz), 1)
             <= lax.broadcasted_iota(jnp.int32, (tsz, tsz), 0)).astype(F32)
    c = carry_ref[0:1, :] + jnp.dot(lower, logf, precision=lax.Precision.HIGHEST,
                                    preferred_element_type=F32)
    carry_ref[...] = carry_ref[...] + jnp.sum(logf, axis=0, keepdims=True)

    bias = -LOG2E * c
    hi = bias.astype(BF16).astype(F32)
    mid = (bias - hi).astype(BF16).astype(F32)
    lo = (bias - hi - mid).astype(BF16).astype(F32)
    pieces = jnp.where(col < FOX_HEADS, hi,
                       jnp.where(col < 2 * FOX_HEADS, mid,
                                 jnp.where(col < 3 * FOX_HEADS, lo, 0.0)))
    removed = jnp.where(col < FOX_HEADS, NEG_BIG, 0.0)
    kaug_ref[...] = jnp.where(valid, pieces, removed).astype(BF16)
    ab_ref[...] = jnp.dot(n, wab_ref[...].astype(BF16), preferred_element_type=F32)


def _small_proj(n, w_f, f_bias, w_ab, *, seq):
    lp, d = n.shape
    heads = w_f.shape[1]
    assert 3 * heads <= LANE
    rank = w_ab.shape[1]
    tsz = META_BLOCK
    n_x = seq // tsz
    fill = LANE - 3 * heads
    wf3 = jnp.concatenate([w_f, w_f, w_f, jnp.zeros((d, fill), w_f.dtype)], axis=1)
    fb3 = jnp.concatenate([f_bias, f_bias, f_bias, jnp.zeros((fill,), f_bias.dtype)])

    def row_block(t):
        return jnp.where(t == 0, n_x, t - 1)

    return pl.pallas_call(
        _small_proj_body,
        grid=(n_x + 1,),
        in_specs=[pl.BlockSpec((tsz, d), lambda t: (row_block(t), 0)),
                  pl.BlockSpec((d, LANE), lambda t: (0, 0)),
                  pl.BlockSpec((1, LANE), lambda t: (0, 0)),
                  pl.BlockSpec((d, rank), lambda t: (0, 0))],
        out_specs=[pl.BlockSpec((tsz, LANE), lambda t: (row_block(t), 0)),
                   pl.BlockSpec((tsz, rank), lambda t: (row_block(t), 0))],
        out_shape=[jax.ShapeDtypeStruct((lp, LANE), BF16),
                   jax.ShapeDtypeStruct((lp, rank), F32)],
        scratch_shapes=[pltpu.VMEM((8, LANE), F32)],
        compiler_params=pltpu.CompilerParams(
            dimension_semantics=("arbitrary",), vmem_limit_bytes=VMEM_LIMIT),
        name="small_proj",
    )(n, wf3, fb3.reshape(1, LANE), w_ab)


def _fox_body(tq, rq, i_tab, j_tab, q_ref, km_ref, vm_ref, am_ref, kx_ref, vx_ref,
              ax_ref, o_ref, m_ref, acc_ref):
    h = pl.program_id(0)
    t = pl.program_id(1)
    i = i_tab[t]
    j = j_tab[t]
    dh = q_ref.shape[1]
    lane = lax.broadcasted_iota(jnp.int32, (rq, LANE), 1)
    pick = jnp.logical_or(lane == h, jnp.logical_or(lane == h + FOX_HEADS,
                                                    lane == h + 2 * FOX_HEADS))
    q_ones = jnp.where(pick, 1.0, 0.0).astype(BF16)

    def online_update(k_ref, v_ref, a_ref, diagonal):
        n_k = k_ref.shape[0]
        keys = jnp.concatenate([k_ref[...], a_ref[...]], axis=1)
        vals = jnp.concatenate([v_ref[...], jnp.ones((n_k, LANE), BF16)], axis=1)
        for r in range(tq // rq):
            rows = pl.ds(r * rq, rq)
            n_c = (r + 1) * rq if diagonal else n_k
            q = jnp.concatenate([q_ref[rows, :], q_ones], axis=1)
            s = lax.dot_general(q, keys[:n_c], (((1,), (1,)), ((), ())),
                                preferred_element_type=F32)
            if diagonal:
                tri = (lax.broadcasted_iota(jnp.int32, (rq, rq), 1)
                       <= lax.broadcasted_iota(jnp.int32, (rq, rq), 0))
                s_last = jnp.where(tri, s[:, n_c - rq:], NEG_BIG)
                s = s_last if r == 0 else jnp.concatenate(
                    [s[:, :n_c - rq], s_last], axis=1)
            m_prev = m_ref[rows, :]
            m_next = jnp.maximum(m_prev, jnp.max(s, axis=1, keepdims=True))
            p = jnp.exp2(s - jnp.tile(m_next, (1, n_c // LANE)))
            alpha = jnp.exp2(m_prev - m_next)
            m_ref[rows, :] = m_next
            acc_ref[rows, :] = (jnp.tile(alpha, (1, 2)) * acc_ref[rows, :]
                                + jnp.dot(p.astype(BF16), vals[:n_c],
                                          preferred_element_type=F32))

    @pl.when(j == 0)
    def _():
        m_ref[...] = jnp.full_like(m_ref, NEG_BIG)
        acc_ref[...] = jnp.zeros_like(acc_ref)
        online_update(km_ref, vm_ref, am_ref, False)

    @pl.when(jnp.logical_and(j >= 1, j - 1 < i))
    def _():
        online_update(kx_ref, vx_ref, ax_ref, False)

    @pl.when(j - 1 == i)
    def _():
        online_update(kx_ref, vx_ref, ax_ref, True)
        acc = acc_ref[...]
        o_ref[...] = (acc[:, :dh] / acc[:, dh:]).astype(o_ref.dtype)


def _fox_attention(qkv, kaug, *, seq, tq, rq):
    heads = FOX_HEADS
    dh = FOX_HEAD_DIM
    assert dh == LANE and seq % tq == 0 and tq % rq == 0 and rq % LANE == 0
    n_q = seq // tq
    meta_blk = seq // META_BLOCK
    steps = [(i, j) for i in range(n_q) for j in range(i + 2)]
    i_tab = jnp.asarray([s[0] for s in steps], jnp.int32)
    j_tab = jnp.asarray([s[1] for s in steps], jnp.int32)

    def kv_block(t, j_tab):
        return jnp.maximum(j_tab[t] - 1, 0)

    grid_spec = pltpu.PrefetchScalarGridSpec(
        num_scalar_prefetch=2,
        grid=(heads, len(steps)),
        in_specs=[
            pl.BlockSpec((tq, dh), lambda h, t, it, jt: (it[t], h)),
            pl.BlockSpec((META_BLOCK, dh), lambda h, t, it, jt: (meta_blk, heads + h)),
            pl.BlockSpec((META_BLOCK, dh),
                         lambda h, t, it, jt: (meta_blk, 2 * heads + h)),
            pl.BlockSpec((META_BLOCK, LANE), lambda h, t, it, jt: (meta_blk, 0)),
            pl.BlockSpec((tq, dh), lambda h, t, it, jt: (kv_block(t, jt), heads + h)),
            pl.BlockSpec((tq, dh),
                         lambda h, t, it, jt: (kv_block(t, jt), 2 * heads + h)),
            pl.BlockSpec((tq, LANE), lambda h, t, it, jt: (kv_block(t, jt), 0)),
        ],
        out_specs=pl.BlockSpec((tq, dh), lambda h, t, it, jt: (it[t], h)),
        scratch_shapes=[pltpu.VMEM((tq, LANE), F32), pltpu.VMEM((tq, 2 * dh), F32)],
    )
    return pl.pallas_call(
        functools.partial(_fox_body, tq, rq),
        grid_spec=grid_spec,
        out_shape=jax.ShapeDtypeStruct((seq, heads * dh), BF16),
        compiler_params=pltpu.CompilerParams(
            dimension_semantics=("parallel", "arbitrary"),
            vmem_limit_bytes=VMEM_LIMIT),
        name="fox_attention",
    )(i_tab, j_tab, qkv, qkv, qkv, kaug, qkv, qkv, kaug)


def _gla_body(qv_ref, kr_ref, ab_ref, w2_ref, b_ref, g_ref, o_ref, state_ref):
    t = pl.program_id(0)
    is_meta = t == 0

    @pl.when(is_meta)
    def _():
        state_ref[...] = jnp.zeros_like(state_ref)

    z = jnp.dot(ab_ref[...].astype(BF16), w2_ref[...].astype(BF16),
                preferred_element_type=F32) + b_ref[...]
    la = _log_sigmoid(z) / GLA_GATE_TAU
    row = lax.broadcasted_iota(jnp.int32, la.shape, 0)
    la = jnp.where(jnp.logical_and(is_meta, row >= N_META), 0.0, la)
    c = la.shape[0]
    lower = (lax.broadcasted_iota(jnp.int32, (c, c), 1)
             <= lax.broadcasted_iota(jnp.int32, (c, c), 0)).astype(F32)
    cum = jnp.dot(lower, la, precision=lax.Precision.HIGHEST,
                  preferred_element_type=F32)
    total = cum[c - 1:c, :]
    k_dec = (kr_ref[:, :GLA_KW] * jnp.exp(total - cum)).astype(BF16)
    decay = jnp.exp(total)

    for h in range(GLA_HEADS):
        ks = slice(h * GLA_DK, (h + 1) * GLA_DK)
        vs = slice(GLA_KW + h * GLA_DV, GLA_KW + (h + 1) * GLA_DV)
        kv = lax.dot_general(qv_ref[:, vs], k_dec[:, ks], (((0,), (0,)), ((), ())),
                             preferred_element_type=F32)
        s_new = state_ref[h] * decay[:, ks] + kv
        state_ref[h] = s_new
        o = lax.dot_general(qv_ref[:, ks], s_new.astype(BF16),
                            (((1,), (1,)), ((), ())),
                            preferred_element_type=F32)
        ms = jnp.mean(o * o, axis=-1, keepdims=True)
        r = kr_ref[:, vs]
        o_ref[:, h * GLA_DV:(h + 1) * GLA_DV] = (
            o * lax.rsqrt(ms + EPS) * g_ref[...] * (r * _sigmoid(r))
        ).astype(o_ref.dtype)


def _gla(qv, kr, ab, w2, b, g, *, seq):
    n_x = seq // CHUNK

    def row_block(t):
        return jnp.where(t == 0, n_x, t - 1)

    width = GLA_KW + GLA_VW
    return pl.pallas_call(
        _gla_body,
        grid=(n_x + 1,),
        in_specs=[pl.BlockSpec((CHUNK, width), lambda t: (row_block(t), 0)),
                  pl.BlockSpec((CHUNK, width), lambda t: (row_block(t), 0)),
                  pl.BlockSpec((CHUNK, GLA_GATE_RANK), lambda t: (row_block(t), 0)),
                  pl.BlockSpec((GLA_GATE_RANK, GLA_KW), lambda t: (0, 0)),
                  pl.BlockSpec((1, GLA_KW), lambda t: (0, 0)),
                  pl.BlockSpec((1, GLA_DV), lambda t: (0, 0))],
        out_specs=pl.BlockSpec((CHUNK, GLA_VW), lambda t: (jnp.maximum(t - 1, 0), 0)),
        out_shape=jax.ShapeDtypeStruct((seq, GLA_VW), BF16),
        scratch_shapes=[pltpu.VMEM((GLA_HEADS, GLA_DV, GLA_DK), F32)],
        compiler_params=pltpu.CompilerParams(
            dimension_semantics=("arbitrary",), vmem_limit_bytes=VMEM_LIMIT),
        name="gla",
    )(qv, kr, ab, w2, b.reshape(1, GLA_KW), g.reshape(1, GLA_DV))


def kernel(x, meta_tokens, norm1_g, ffn1_w1, ffn1_w3, ffn1_w2, norm_mix_g, w_in,
           fox_f_bias, gla_alpha_w2, gla_alpha_b, gla_norm_g, w_gate, w_proj_fox,
           w_proj_gla, w_out, norm2_g, ffn2_w1, ffn2_w3, ffn2_w2, norm_final_g):
    batch, seq, d = x.shape
    assert batch == 1 and seq % 1024 == 0
    depth = norm1_g.shape[0]
    lp = seq + META_BLOCK
    assert lp % 8 == 0 and (lp // 8) % 16 == 0
    tm_all = lp // 8
    tm_x = seq // 8

    pad = jnp.zeros((META_BLOCK - N_META, d), x.dtype)
    h = jnp.concatenate([x[0], meta_tokens.astype(x.dtype), pad], axis=0)

    o_fa = FOX_WIDTH * 3
    o_qb = o_fa + FOX_HEADS
    o_kb = o_qb + GLA_KW
    o_vb = o_kb + GLA_KW
    o_rb = o_vb + GLA_VW
    o_ab = o_rb + GLA_VW
    fox_scale = FOX_HEAD_DIM ** -0.5 * LOG2E
    gla_scale = GLA_DK ** -0.5

    assert depth == 1
    for li in range(depth):
        n1 = _rmsnorm(h, norm1_g[li], lp, BF16)
        h = _swiglu_ffn(n1, h, ffn1_w1[li], ffn1_w3[li], ffn1_w2[li], m_rows=lp,
                        tm_up=tm_all, tm_down=2 * tm_all)

        n = _rmsnorm(h, norm_mix_g[li], lp, BF16)
        w = w_in[li]

        tn_p = 512

        def fox_epilogue(dots, extras, j):
            scale = jnp.where(j < FOX_WIDTH // tn_p, fox_scale, 1.0)
            return dots[0] * scale

        qkv_a = _matmul_fullk([n], [(0, w, 0)], [], fox_epilogue, m_rows=lp,
                              n_cols=3 * FOX_WIDTH, out_dtype=BF16,
                              tm=tm_all, tn=tn_p, name="proj_fox")

        w_qv = jnp.concatenate([w[:, o_qb:o_kb], w[:, o_vb:o_rb]], axis=1)
        w_kr = jnp.concatenate([w[:, o_kb:o_vb], w[:, o_rb:o_ab]], axis=1)

        def gla_q_epilogue(dots, extras, j):
            scale = jnp.where(j < GLA_KW // tn_p, gla_scale, 1.0)
            return dots[0] * scale

        qv_b = _matmul_fullk([n], [(0, w_qv, 0)], [], gla_q_epilogue, m_rows=lp,
                             n_cols=GLA_KW + GLA_VW, out_dtype=BF16,
                             tm=tm_all, tn=tn_p, name="proj_gla_qv")
        kr_b = _matmul_fullk([n], [(0, w_kr, 0)], [], lambda a, e, j: a[0], m_rows=lp,
                             n_cols=GLA_KW + GLA_VW, out_dtype=F32,
                             tm=tm_all, tn=tn_p, name="proj_gla_kr")

        kaug, ab = _small_proj(n, w[:, o_fa:o_qb], fox_f_bias[li], w[:, o_ab:], seq=seq)
        o_a = _fox_attention(qkv_a, kaug, seq=seq, tq=2048, rq=256)
        o_b = _gla(qv_b, kr_b, ab, gla_alpha_w2[li], gla_alpha_b[li], gla_norm_g[li],
                   seq=seq)

        wg = w_gate[li]

        def mix_epilogue(accs, extras, j):
            ga, gb, pa, pb = accs
            return _sigmoid(ga) * pa + _sigmoid(gb) * pb

        tn_m = 256
        y = _matmul_fullk(
            [n, o_a, o_b],
            [(0, wg, 0), (0, wg, d // tn_m), (1, w_proj_fox[li], 0),
             (2, w_proj_gla[li], 0)],
            [], mix_epilogue, m_rows=seq, n_cols=d, out_dtype=BF16,
            tm=tm_x, tn=tn_m, name="mix", a_buffers=1)

        h = _matmul_fullk([y], [(0, w_out[li], 0)], [h],
                          lambda a, e, j: e[0] + a[0], m_rows=seq, n_cols=d,
                          out_dtype=F32, tm=tm_x, tn=512, name="out_proj")

        n2 = _rmsnorm(h, norm2_g[li], seq, BF16)
        h = _swiglu_ffn(n2, h, ffn2_w1[li], ffn2_w3[li], ffn2_w2[li], m_rows=seq,
                        tm_up=tm_x, tm_down=2 * tm_x)

    out = _rmsnorm(h, norm_final_g, seq, F32)
    return out.reshape(batch, seq, d)
```

```python
import functools

import jax
import jax.numpy as jnp
from jax import lax
from jax.experimental import pallas as pl
from jax.experimental.pallas import tpu as pltpu

F32 = jnp.float32
BF16 = jnp.bfloat16

EPS = 1e-6
N_META = 16
CHUNK = 64
FOX_HEADS = 16
FOX_HEAD_DIM = 128
FOX_WIDTH = FOX_HEADS * FOX_HEAD_DIM
GLA_HEADS = 4
GLA_DK = 256
GLA_DV = 512
GLA_KW = GLA_HEADS * GLA_DK
GLA_VW = GLA_HEADS * GLA_DV
GLA_GATE_RANK = 16
GLA_GATE_TAU = 16.0

LANE = 128
META_BLOCK = LANE
NEG_BIG = -1e30
LOG2E = 1.4426950408889634
VMEM_LIMIT = 60 * 1024 * 1024


def _log_sigmoid(z):
    return jnp.minimum(z, 0.0) - jnp.log(1.0 + jnp.exp(-jnp.abs(z)))


def _sigmoid(z):
    return 1.0 / (1.0 + jnp.exp(-z))


def _rmsnorm_body(x_ref, g_ref, o_ref):
    x = x_ref[...]
    ms = jnp.mean(x * x, axis=-1, keepdims=True)
    o_ref[...] = (x * lax.rsqrt(ms + EPS) * g_ref[...]).astype(o_ref.dtype)


def _rmsnorm(x, g, rows, out_dtype):
    d = x.shape[1]
    tile = max(t for t in range(16, 513, 16) if rows % t == 0)
    return pl.pallas_call(
        _rmsnorm_body,
        grid=(rows // tile,),
        in_specs=[pl.BlockSpec((tile, d), lambda i: (i, 0)),
                  pl.BlockSpec((1, d), lambda i: (0, 0))],
        out_specs=pl.BlockSpec((tile, d), lambda i: (i, 0)),
        out_shape=jax.ShapeDtypeStruct((rows, d), out_dtype),
        compiler_params=pltpu.CompilerParams(
            dimension_semantics=("parallel",), vmem_limit_bytes=VMEM_LIMIT),
        name="rmsnorm",
    )(x, g.reshape(1, d))


def _rmsnorm_frames_meta_body(n_x, x_ref, mp_ref, g_ref, o_ref):
    src = jnp.where(pl.program_id(0) < n_x, x_ref[...], mp_ref[...])
    ms = jnp.mean(src * src, axis=-1, keepdims=True)
    o_ref[...] = (src * lax.rsqrt(ms + EPS) * g_ref[...]).astype(o_ref.dtype)


def _rmsnorm_frames_meta(x, meta_pad, g, *, lp):
    seq, d = x.shape
    tile = meta_pad.shape[0]
    assert seq % tile == 0 and seq < lp <= seq + tile
    n_x = seq // tile
    return pl.pallas_call(
        functools.partial(_rmsnorm_frames_meta_body, n_x),
        grid=(n_x + 1,),
        in_specs=[pl.BlockSpec((tile, d), lambda i: (jnp.minimum(i, n_x - 1), 0)),
                  pl.BlockSpec((tile, d), lambda i: (0, 0)),
                  pl.BlockSpec((1, d), lambda i: (0, 0))],
        out_specs=pl.BlockSpec((tile, d), lambda i: (i, 0)),
        out_shape=jax.ShapeDtypeStruct((lp, d), BF16),
        compiler_params=pltpu.CompilerParams(
            dimension_semantics=("parallel",), vmem_limit_bytes=VMEM_LIMIT),
        name="rmsnorm_in",
    )(x, meta_pad, g.reshape(1, d))


def _mm_fullk_body(pair_ai, n_a, n_extra, epilogue, *refs):
    n_p = len(pair_ai)
    a_refs = refs[:n_a]
    b_refs = refs[n_a:n_a + n_p]
    e_refs = refs[n_a + n_p:n_a + n_p + n_extra]
    o_ref = refs[n_a + n_p + n_extra]
    dots = [jnp.dot(a_refs[ai][...], b_refs[p][...].astype(BF16),
                    preferred_element_type=F32) for p, ai in enumerate(pair_ai)]
    extras = [r[...] for r in e_refs]
    o_ref[...] = epilogue(dots, extras, pl.program_id(1)).astype(o_ref.dtype)


def _matmul_fullk(a_list, pairs, extras, epilogue, *, m_rows, n_cols, out_dtype,
                  tm, tn, name, a_buffers=2):
    assert m_rows % tm == 0 and n_cols % tn == 0
    a_mode = {} if a_buffers == 2 else {"pipeline_mode": pl.Buffered(a_buffers)}
    in_specs = [pl.BlockSpec((tm, a.shape[1]), lambda i, j: (i, 0), **a_mode)
                for a in a_list]
    for ai, b, col_off in pairs:
        assert b.shape[0] == a_list[ai].shape[1]
        in_specs.append(pl.BlockSpec((b.shape[0], tn),
                                     lambda i, j, col_off=col_off: (0, j + col_off)))
    for _ in extras:
        in_specs.append(pl.BlockSpec((tm, tn), lambda i, j: (i, j)))
    body = functools.partial(_mm_fullk_body, tuple(ai for ai, _, _ in pairs),
                             len(a_list), len(extras), epilogue)
    return pl.pallas_call(
        body,
        grid=(m_rows // tm, n_cols // tn),
        in_specs=in_specs,
        out_specs=pl.BlockSpec((tm, tn), lambda i, j: (i, j)),
        out_shape=jax.ShapeDtypeStruct((m_rows, n_cols), out_dtype),
        compiler_params=pltpu.CompilerParams(
            dimension_semantics=("parallel", "arbitrary"),
            vmem_limit_bytes=VMEM_LIMIT),
        name=name,
    )(*a_list, *[b for _, b, _ in pairs], *extras)


def _mm_shifted_body(shift, epilogue, tab_ref, a_ref, bm_ref, bn_ref, o_ref):
    del tab_ref
    tn = bm_ref.shape[1]
    b = jnp.concatenate([bm_ref[...], bn_ref[...]], axis=1)
    rolled = [pltpu.roll(b[:, c * LANE:(c + 1) * LANE], LANE - shift, 1)
              for c in range(tn // LANE + 1)]
    keep = lax.broadcasted_iota(jnp.int32, (1, LANE), 1) < LANE - shift
    cols = [jnp.where(keep, rolled[c], rolled[c + 1]) for c in range(tn // LANE)]
    bs = jnp.concatenate(cols, axis=1).astype(BF16)
    d = jnp.dot(a_ref[...], bs, preferred_element_type=F32)
    o_ref[...] = epilogue(d, pl.program_id(1)).astype(o_ref.dtype)


def _matmul_shifted(a, b, tile_starts, shift, epilogue, *, m_rows, out_dtype, tm, tn,
                    name):
    assert m_rows % tm == 0 and all(s % tn == 0 for s in tile_starts)
    k_total = a.shape[1]
    tab = jnp.asarray([s // tn for s in tile_starts], jnp.int32)
    per = tn // LANE
    grid_spec = pltpu.PrefetchScalarGridSpec(
        num_scalar_prefetch=1,
        grid=(m_rows // tm, len(tile_starts)),
        in_specs=[pl.BlockSpec((tm, k_total), lambda i, j, tab: (i, 0)),
                  pl.BlockSpec((k_total, tn), lambda i, j, tab: (0, tab[j])),
                  pl.BlockSpec((k_total, LANE),
                               lambda i, j, tab: (0, (tab[j] + 1) * per))],
        out_specs=pl.BlockSpec((tm, tn), lambda i, j, tab: (i, j)),
    )
    return pl.pallas_call(
        functools.partial(_mm_shifted_body, shift, epilogue),
        grid_spec=grid_spec,
        out_shape=jax.ShapeDtypeStruct((m_rows, tn * len(tile_starts)), out_dtype),
        compiler_params=pltpu.CompilerParams(
            dimension_semantics=("parallel", "arbitrary"),
            vmem_limit_bytes=VMEM_LIMIT),
        name=name,
    )(tab, a, b, b)


def _mm_kacc_body(tk, k_total, scale, has_tail, a_ref, b_ref, r_ref, *rest):
    if has_tail:
        tail_ref, o_ref = rest
    else:
        (o_ref,) = rest
    k = pl.program_id(2)
    last_row_block = pl.program_id(0) == pl.num_programs(0) - 1

    def residual():
        r = r_ref[...]
        if has_tail:
            tm, t_rows = r.shape[0], tail_ref.shape[0]
            tail = jnp.concatenate(
                [jnp.zeros((tm - t_rows, r.shape[1]), r.dtype), tail_ref[...]], axis=0)
            row = lax.broadcasted_iota(jnp.int32, r.shape, 0)
            in_tail = jnp.logical_and(last_row_block, row >= tm - t_rows)
            r = jnp.where(in_tail, tail, r)
        return r

    def partial_product():
        a = a_ref[...]
        b = b_ref[...]
        if k_total % tk:
            limit = k_total - k * tk
            a = jnp.where(lax.broadcasted_iota(jnp.int32, a.shape, 1) < limit, a,
                          jnp.zeros_like(a))
            b = jnp.where(lax.broadcasted_iota(jnp.int32, b.shape, 0) < limit, b,
                          jnp.zeros_like(b))
        return scale * jnp.dot(a, b.astype(BF16), preferred_element_type=F32)

    @pl.when(k == 0)
    def _():
        o_ref[...] = residual() + partial_product()

    @pl.when(k > 0)
    def _():
        o_ref[...] += partial_product()


def _matmul_kacc(a, b, res, scale, *, m_rows, tm, tn, tk, name, res_tail=None):
    k_total, n_cols = b.shape
    assert a.shape[1] == k_total and m_rows % tm == 0 and n_cols % tn == 0
    in_specs = [pl.BlockSpec((tm, tk), lambda i, j, k: (i, k)),
                pl.BlockSpec((tk, tn), lambda i, j, k: (k, j)),
                pl.BlockSpec((tm, tn), lambda i, j, k: (i, j))]
    operands = [a, b, res]
    if res_tail is not None:
        assert res.shape[0] + res_tail.shape[0] == m_rows
        in_specs.append(pl.BlockSpec((res_tail.shape[0], tn), lambda i, j, k: (0, j)))
        operands.append(res_tail)
    return pl.pallas_call(
        functools.partial(_mm_kacc_body, tk, k_total, scale, res_tail is not None),
        grid=(m_rows // tm, n_cols // tn, pl.cdiv(k_total, tk)),
        in_specs=in_specs,
        out_specs=pl.BlockSpec((tm, tn), lambda i, j, k: (i, j)),
        out_shape=jax.ShapeDtypeStruct((m_rows, n_cols), F32),
        compiler_params=pltpu.CompilerParams(
            dimension_semantics=("parallel", "parallel", "arbitrary"),
            vmem_limit_bytes=VMEM_LIMIT),
        name=name,
    )(*operands)


def _swiglu_ffn(n, h_res, w1, w3, w2, *, m_rows, tm_up, tm_down, res_tail=None):
    def up_epilogue(dots, extras, j):
        a1, a3 = dots
        return a1 * _sigmoid(a1) * a3

    u = _matmul_fullk([n], [(0, w1, 0), (0, w3, 0)], [], up_epilogue,
                      m_rows=m_rows, n_cols=w1.shape[1], out_dtype=BF16,
                      tm=tm_up, tn=256, name="ffn_up")
    return _matmul_kacc(u, w2, h_res, 0.5, m_rows=m_rows, tm=tm_down, tn=1024, tk=1024,
                        name="ffn_down", res_tail=res_tail)


def _small_proj_body(f_off, ab_off, n_ref, wf_ref, fb_ref, wab_ref, kaug_ref, ab_ref,
                     carry_ref):
    t = pl.program_id(0)
    is_meta = t == 0

    @pl.when(is_meta)
    def _():
        carry_ref[...] = jnp.zeros_like(carry_ref)

    n = n_ref[...]
    ft = jnp.dot(n, wf_ref[...].astype(BF16), preferred_element_type=F32)
    row = lax.broadcasted_iota(jnp.int32, ft.shape, 0)
    col = lax.broadcasted_iota(jnp.int32, ft.shape, 1)
    ft = jnp.where(jnp.logical_and(col >= f_off, col < f_off + FOX_HEADS), ft, 0.0)
    shifts = [(p * FOX_HEADS - f_off) % LANE for p in range(3)]
    ft = sum(pltpu.roll(ft, s, 1) if s else ft for s in shifts)
    logf = _log_sigmoid(ft + fb_ref[...])
    valid = jnp.logical_or(jnp.logical_not(is_meta), row < N_META)
    logf = jnp.where(valid, logf, 0.0)
    tsz = logf.shape[0]
    lower = (lax.broadcasted_iota(jnp.int32, (tsz, tsz), 1)
             <= lax.broadcasted_iota(jnp.int32, (tsz, tsz), 0)).astype(F32)
    c = carry_ref[0:1, :] + jnp.dot(lower, logf, precision=lax.Precision.HIGHEST,
                                    preferred_element_type=F32)
    carry_ref[...] = carry_ref[...] + jnp.sum(logf, axis=0, keepdims=True)

    bias = -LOG2E * c
    hi = bias.astype(BF16).astype(F32)
    mid = (bias - hi).astype(BF16).astype(F32)
    lo = (bias - hi - mid).astype(BF16).astype(F32)
    pieces = jnp.where(col < FOX_HEADS, hi,
                       jnp.where(col < 2 * FOX_HEADS, mid,
                                 jnp.where(col < 3 * FOX_HEADS, lo, 0.0)))
    removed = jnp.where(col < FOX_HEADS, NEG_BIG, 0.0)
    kaug_ref[...] = jnp.where(valid, pieces, removed).astype(BF16)
    wcol = lax.broadcasted_iota(jnp.int32, wab_ref.shape, 1)
    wab = jnp.where(jnp.logical_and(wcol >= ab_off, wcol < ab_off + GLA_GATE_RANK),
                    wab_ref[...], 0.0)
    ab_ref[...] = jnp.dot(n, wab.astype(BF16), preferred_element_type=F32)


def _small_proj(n, w, f_bias, *, seq, f_col, ab_col):
    lp, d = n.shape
    heads = f_bias.shape[0]
    assert heads == FOX_HEADS and 3 * heads <= LANE
    assert f_col % LANE + heads <= LANE and ab_col % LANE + GLA_GATE_RANK <= LANE
    tsz = META_BLOCK
    n_x = seq // tsz
    fb3 = jnp.concatenate([f_bias, f_bias, f_bias,
                           jnp.zeros((LANE - 3 * heads,), f_bias.dtype)])

    def row_block(t):
        return jnp.where(t == 0, n_x, t - 1)

    return pl.pallas_call(
        functools.partial(_small_proj_body, f_col % LANE, ab_col % LANE),
        grid=(n_x + 1,),
        in_specs=[pl.BlockSpec((tsz, d), lambda t: (row_block(t), 0)),
                  pl.BlockSpec((d, LANE), lambda t: (0, f_col // LANE)),
                  pl.BlockSpec((1, LANE), lambda t: (0, 0)),
                  pl.BlockSpec((d, LANE), lambda t: (0, ab_col // LANE))],
        out_specs=[pl.BlockSpec((tsz, LANE), lambda t: (row_block(t), 0)),
                   pl.BlockSpec((tsz, LANE), lambda t: (row_block(t), 0))],
        out_shape=[jax.ShapeDtypeStruct((lp, LANE), BF16),
                   jax.ShapeDtypeStruct((lp, LANE), F32)],
        scratch_shapes=[pltpu.VMEM((8, LANE), F32)],
        compiler_params=pltpu.CompilerParams(
            dimension_semantics=("arbitrary",), vmem_limit_bytes=VMEM_LIMIT),
        name="small_proj",
    )(n, w, fb3.reshape(1, LANE), w)


def _fox_body(tq, rq, i_tab, j_tab, q_ref, km_ref, vm_ref, am_ref, kx_ref, vx_ref,
              ax_ref, o_ref, m_ref, acc_ref):
    h = pl.program_id(0)
    t = pl.program_id(1)
    i = i_tab[t]
    j = j_tab[t]
    dh = q_ref.shape[1]
    lane = lax.broadcasted_iota(jnp.int32, (rq, LANE), 1)
    pick = jnp.logical_or(lane == h, jnp.logical_or(lane == h + FOX_HEADS,
                                                    lane == h + 2 * FOX_HEADS))
    q_ones = jnp.where(pick, 1.0, 0.0).astype(BF16)

    def online_update(k_ref, v_ref, a_ref, diagonal):
        n_k = k_ref.shape[0]
        keys = jnp.concatenate([k_ref[...], a_ref[...]], axis=1)
        vals = jnp.concatenate([v_ref[...], jnp.ones((n_k, LANE), BF16)], axis=1)
        for r in range(tq // rq):
            rows = pl.ds(r * rq, rq)
            n_c = (r + 1) * rq if diagonal else n_k
            q = jnp.concatenate([q_ref[rows, :], q_ones], axis=1)
            s = lax.dot_general(q, keys[:n_c], (((1,), (1,)), ((), ())),
                                preferred_element_type=F32)
            if diagonal:
                tri = (lax.broadcasted_iota(jnp.int32, (rq, rq), 1)
                       <= lax.broadcasted_iota(jnp.int32, (rq, rq), 0))
                s_last = jnp.where(tri, s[:, n_c - rq:], NEG_BIG)
                s = s_last if r == 0 else jnp.concatenate(
                    [s[:, :n_c - rq], s_last], axis=1)
            m_prev = m_ref[rows, :]
            m_next = jnp.maximum(m_prev, jnp.max(s, axis=1, keepdims=True))
            p = jnp.exp2(s - jnp.tile(m_next, (1, n_c // LANE)))
            alpha = jnp.exp2(m_prev - m_next)
            m_ref[rows, :] = m_next
            acc_ref[rows, :] = (jnp.tile(alpha, (1, 2)) * acc_ref[rows, :]
                                + jnp.dot(p.astype(BF16), vals[:n_c],
                                          preferred_element_type=F32))

    @pl.when(j == 0)
    def _():
        m_ref[...] = jnp.full_like(m_ref, NEG_BIG)
        acc_ref[...] = jnp.zeros_like(acc_ref)
        online_update(km_ref, vm_ref, am_ref, False)

    @pl.when(jnp.logical_and(j >= 1, j - 1 < i))
    def _():
        online_update(kx_ref, vx_ref, ax_ref, False)

    @pl.when(j - 1 == i)
    def _():
        online_update(kx_ref, vx_ref, ax_ref, True)
        acc = acc_ref[...]
        o_ref[...] = (acc[:, :dh] / acc[:, dh:]).astype(o_ref.dtype)


def _fox_attention(qkv, kaug, *, seq, tq, rq):
    heads = FOX_HEADS
    dh = FOX_HEAD_DIM
    assert dh == LANE and seq % tq == 0 and tq % rq == 0 and rq % LANE == 0
    n_q = seq // tq
    meta_blk = seq // META_BLOCK
    steps = [(i, j) for i in range(n_q) for j in range(i + 2)]
    i_tab = jnp.asarray([s[0] for s in steps], jnp.int32)
    j_tab = jnp.asarray([s[1] for s in steps], jnp.int32)

    def kv_block(t, j_tab):
        return jnp.maximum(j_tab[t] - 1, 0)

    grid_spec = pltpu.PrefetchScalarGridSpec(
        num_scalar_prefetch=2,
        grid=(heads, len(steps)),
        in_specs=[
            pl.BlockSpec((tq, dh), lambda h, t, it, jt: (it[t], h)),
            pl.BlockSpec((META_BLOCK, dh), lambda h, t, it, jt: (meta_blk, heads + h)),
            pl.BlockSpec((META_BLOCK, dh),
                         lambda h, t, it, jt: (meta_blk, 2 * heads + h)),
            pl.BlockSpec((META_BLOCK, LANE), lambda h, t, it, jt: (meta_blk, 0)),
            pl.BlockSpec((tq, dh), lambda h, t, it, jt: (kv_block(t, jt), heads + h)),
            pl.BlockSpec((tq, dh),
                         lambda h, t, it, jt: (kv_block(t, jt), 2 * heads + h)),
            pl.BlockSpec((tq, LANE), lambda h, t, it, jt: (kv_block(t, jt), 0)),
        ],
        out_specs=pl.BlockSpec((tq, dh), lambda h, t, it, jt: (it[t], h)),
        scratch_shapes=[pltpu.VMEM((tq, LANE), F32), pltpu.VMEM((tq, 2 * dh), F32)],
    )
    return pl.pallas_call(
        functools.partial(_fox_body, tq, rq),
        grid_spec=grid_spec,
        out_shape=jax.ShapeDtypeStruct((seq, heads * dh), BF16),
        compiler_params=pltpu.CompilerParams(
            dimension_semantics=("parallel", "arbitrary"),
            vmem_limit_bytes=VMEM_LIMIT),
        name="fox_attention",
    )(i_tab, j_tab, qkv, qkv, qkv, kaug, qkv, qkv, kaug)


def _gla_body(qv_ref, kr_ref, ab_ref, w2_ref, b_ref, g_ref, o_ref, state_ref):
    t = pl.program_id(0)
    is_meta = t == 0

    @pl.when(is_meta)
    def _():
        state_ref[...] = jnp.zeros_like(state_ref)

    z = jnp.dot(ab_ref[...].astype(BF16), w2_ref[...].astype(BF16),
                preferred_element_type=F32) + b_ref[...]
    la = _log_sigmoid(z) / GLA_GATE_TAU
    row = lax.broadcasted_iota(jnp.int32, la.shape, 0)
    la = jnp.where(jnp.logical_and(is_meta, row >= N_META), 0.0, la)
    c = la.shape[0]
    lower = (lax.broadcasted_iota(jnp.int32, (c, c), 1)
             <= lax.broadcasted_iota(jnp.int32, (c, c), 0)).astype(F32)
    cum = jnp.dot(lower, la, precision=lax.Precision.HIGHEST,
                  preferred_element_type=F32)
    total = cum[c - 1:c, :]
    k_dec = (kr_ref[:, :GLA_KW] * jnp.exp(total - cum)).astype(BF16)
    decay = jnp.exp(total)

    for h in range(GLA_HEADS):
        ks = slice(h * GLA_DK, (h + 1) * GLA_DK)
        vs = slice(GLA_KW + h * GLA_DV, GLA_KW + (h + 1) * GLA_DV)
        kv = lax.dot_general(qv_ref[:, vs], k_dec[:, ks], (((0,), (0,)), ((), ())),
                             preferred_element_type=F32)
        s_new = state_ref[h] * decay[:, ks] + kv
        state_ref[h] = s_new
        o = lax.dot_general(qv_ref[:, ks], s_new.astype(BF16),
                            (((1,), (1,)), ((), ())),
                            preferred_element_type=F32)
        ms = jnp.mean(o * o, axis=-1, keepdims=True)
        r = kr_ref[:, vs]
        o_ref[:, h * GLA_DV:(h + 1) * GLA_DV] = (
            o * lax.rsqrt(ms + EPS) * g_ref[...] * (r * _sigmoid(r))
        ).astype(o_ref.dtype)


def _gla(qv, kr, ab, w2, b, g, *, seq, ab_off):
    n_x = seq // CHUNK
    w2 = jnp.zeros((LANE, GLA_KW), w2.dtype).at[ab_off:ab_off + GLA_GATE_RANK].set(w2)

    def row_block(t):
        return jnp.where(t == 0, n_x, t - 1)

    width = GLA_KW + GLA_VW
    return pl.pallas_call(
        _gla_body,
        grid=(n_x + 1,),
        in_specs=[pl.BlockSpec((CHUNK, width), lambda t: (row_block(t), 0)),
                  pl.BlockSpec((CHUNK, width), lambda t: (row_block(t), 0)),
                  pl.BlockSpec((CHUNK, LANE), lambda t: (row_block(t), 0)),
                  pl.BlockSpec((LANE, GLA_KW), lambda t: (0, 0)),
                  pl.BlockSpec((1, GLA_KW), lambda t: (0, 0)),
                  pl.BlockSpec((1, GLA_DV), lambda t: (0, 0))],
        out_specs=pl.BlockSpec((CHUNK, GLA_VW), lambda t: (jnp.maximum(t - 1, 0), 0)),
        out_shape=jax.ShapeDtypeStruct((seq, GLA_VW), BF16),
        scratch_shapes=[pltpu.VMEM((GLA_HEADS, GLA_DV, GLA_DK), F32)],
        compiler_params=pltpu.CompilerParams(
            dimension_semantics=("arbitrary",), vmem_limit_bytes=VMEM_LIMIT),
        name="gla",
    )(qv, kr, ab, w2, b.reshape(1, GLA_KW), g.reshape(1, GLA_DV))


def kernel(x, meta_tokens, norm1_g, ffn1_w1, ffn1_w3, ffn1_w2, norm_mix_g, w_in,
           fox_f_bias, gla_alpha_w2, gla_alpha_b, gla_norm_g, w_gate, w_proj_fox,
           w_proj_gla, w_out, norm2_g, ffn2_w1, ffn2_w3, ffn2_w2, norm_final_g):
    batch, seq, d = x.shape
    assert batch == 1 and seq % 1024 == 0
    depth = norm1_g.shape[0]
    lp = seq + META_BLOCK
    assert lp % 8 == 0 and (lp // 8) % 16 == 0
    tm_all = lp // 8
    tm_x = seq // 8

    norm_tile = 2 * META_BLOCK
    meta_pad = jnp.concatenate(
        [meta_tokens.astype(x.dtype), jnp.zeros((norm_tile - N_META, d), x.dtype)], axis=0)
    x2 = x[0]

    o_fa = FOX_WIDTH * 3
    o_qb = o_fa + FOX_HEADS
    o_kb = o_qb + GLA_KW
    o_vb = o_kb + GLA_KW
    o_rb = o_vb + GLA_VW
    o_ab = o_rb + GLA_VW
    fox_scale = FOX_HEAD_DIM ** -0.5 * LOG2E
    gla_scale = GLA_DK ** -0.5

    assert depth == 1
    for li in range(depth):
        n1 = _rmsnorm_frames_meta(x2, meta_pad, norm1_g[li], lp=lp)
        h = _swiglu_ffn(n1, x2, ffn1_w1[li], ffn1_w3[li], ffn1_w2[li], m_rows=lp,
                        tm_up=tm_all, tm_down=2 * tm_all, res_tail=meta_pad[:META_BLOCK])

        n = _rmsnorm(h, norm_mix_g[li], lp, BF16)
        w = w_in[li]

        tn_p = 512

        def fox_epilogue(dots, extras, j):
            scale = jnp.where(j < FOX_WIDTH // tn_p, fox_scale, 1.0)
            return dots[0] * scale

        qkv_a = _matmul_fullk([n], [(0, w, 0)], [], fox_epilogue, m_rows=lp,
                              n_cols=3 * FOX_WIDTH, out_dtype=BF16,
                              tm=tm_all, tn=tn_p, name="proj_fox")

        tn_g = 256
        shift = o_qb % tn_g

        def tiles(start, width):
            assert (start - shift) % tn_g == 0 and width % tn_g == 0
            return [start - shift + c for c in range(0, width, tn_g)]

        def gla_q_epilogue(dot, j):
            return dot * jnp.where(j < GLA_KW // tn_g, gla_scale, 1.0)

        qv_b = _matmul_shifted(n, w, tiles(o_qb, GLA_KW) + tiles(o_vb, GLA_VW), shift,
                               gla_q_epilogue, m_rows=lp, out_dtype=BF16,
                               tm=tm_all, tn=tn_g, name="proj_gla_qv")
        kr_b = _matmul_shifted(n, w, tiles(o_kb, GLA_KW) + tiles(o_rb, GLA_VW), shift,
                               lambda dot, j: dot, m_rows=lp, out_dtype=F32,
                               tm=tm_all, tn=tn_g, name="proj_gla_kr")

        kaug, ab = _small_proj(n, w, fox_f_bias[li], seq=seq, f_col=o_fa, ab_col=o_ab)
        o_a = _fox_attention(qkv_a, kaug, seq=seq, tq=2048, rq=256)
        o_b = _gla(qv_b, kr_b, ab, gla_alpha_w2[li], gla_alpha_b[li], gla_norm_g[li],
                   seq=seq, ab_off=o_ab % LANE)

        wg = w_gate[li]

        def mix_epilogue(accs, extras, j):
            ga, gb, pa, pb = accs
            return _sigmoid(ga) * pa + _sigmoid(gb) * pb

        tn_m = 256
        y = _matmul_fullk(
            [n, o_a, o_b],
            [(0, wg, 0), (0, wg, d // tn_m), (1, w_proj_fox[li], 0),
             (2, w_proj_gla[li], 0)],
            [], mix_epilogue, m_rows=seq, n_cols=d, out_dtype=BF16,
            tm=tm_x, tn=tn_m, name="mix", a_buffers=1)

        h = _matmul_fullk([y], [(0, w_out[li], 0)], [h],
                          lambda a, e, j: e[0] + a[0], m_rows=seq, n_cols=d,
                          out_dtype=F32, tm=tm_x, tn=512, name="out_proj")

        n2 = _rmsnorm(h, norm2_g[li], seq, BF16)
        h = _swiglu_ffn(n2, h, ffn2_w1[li], ffn2_w3[li], ffn2_w2[li], m_rows=seq,
                        tm_up=tm_x, tm_down=2 * tm_x)

    out = _rmsnorm(h, norm_final_g, seq, F32)
    return out.reshape(batch, seq, d)
```

```python
import functools

import jax
import jax.numpy as jnp
from jax import lax
from jax.experimental import pallas as pl
from jax.experimental.pallas import tpu as pltpu

F32 = jnp.float32
BF16 = jnp.bfloat16

EPS = 1e-6
N_META = 16
CHUNK = 64
FOX_HEADS = 16
FOX_HEAD_DIM = 128
FOX_WIDTH = FOX_HEADS * FOX_HEAD_DIM
GLA_HEADS = 4
GLA_DK = 256
GLA_DV = 512
GLA_KW = GLA_HEADS * GLA_DK
GLA_VW = GLA_HEADS * GLA_DV
GLA_GATE_RANK = 16
GLA_GATE_TAU = 16.0

LANE = 128
META_BLOCK = LANE
NEG_BIG = -1e30
LOG2E = 1.4426950408889634
VMEM_LIMIT = 60 * 1024 * 1024


def _log_sigmoid(z):
    return jnp.minimum(z, 0.0) - jnp.log(1.0 + jnp.exp(-jnp.abs(z)))


def _sigmoid(z):
    return 1.0 / (1.0 + jnp.exp(-z))


def _rmsnorm_body(x_ref, g_ref, o_ref):
    x = x_ref[...]
    ms = jnp.mean(x * x, axis=-1, keepdims=True)
    o_ref[...] = (x * lax.rsqrt(ms + EPS) * g_ref[...]).astype(o_ref.dtype)


def _rmsnorm(x, g, rows, out_dtype):
    d = x.shape[1]
    tile = max(t for t in range(16, 513, 16) if rows % t == 0)
    return pl.pallas_call(
        _rmsnorm_body,
        grid=(rows // tile,),
        in_specs=[pl.BlockSpec((tile, d), lambda i: (i, 0)),
                  pl.BlockSpec((1, d), lambda i: (0, 0))],
        out_specs=pl.BlockSpec((tile, d), lambda i: (i, 0)),
        out_shape=jax.ShapeDtypeStruct((rows, d), out_dtype),
        compiler_params=pltpu.CompilerParams(
            dimension_semantics=("parallel",), vmem_limit_bytes=VMEM_LIMIT),
        name="rmsnorm",
    )(x, g.reshape(1, d))


def _rmsnorm_frames_meta_body(n_x, x_ref, mp_ref, g_ref, o_ref):
    src = jnp.where(pl.program_id(0) < n_x, x_ref[...], mp_ref[...])
    ms = jnp.mean(src * src, axis=-1, keepdims=True)
    o_ref[...] = (src * lax.rsqrt(ms + EPS) * g_ref[...]).astype(o_ref.dtype)


def _rmsnorm_frames_meta(x, meta_pad, g, *, lp):
    seq, d = x.shape
    tile = meta_pad.shape[0]
    assert seq % tile == 0 and seq < lp <= seq + tile
    n_x = seq // tile
    return pl.pallas_call(
        functools.partial(_rmsnorm_frames_meta_body, n_x),
        grid=(n_x + 1,),
        in_specs=[pl.BlockSpec((tile, d), lambda i: (jnp.minimum(i, n_x - 1), 0)),
                  pl.BlockSpec((tile, d), lambda i: (0, 0)),
                  pl.BlockSpec((1, d), lambda i: (0, 0))],
        out_specs=pl.BlockSpec((tile, d), lambda i: (i, 0)),
        out_shape=jax.ShapeDtypeStruct((lp, d), BF16),
        compiler_params=pltpu.CompilerParams(
            dimension_semantics=("parallel",), vmem_limit_bytes=VMEM_LIMIT),
        name="rmsnorm_in",
    )(x, meta_pad, g.reshape(1, d))


def _mm_fullk_body(pair_ai, n_a, n_extra, epilogue, *refs):
    n_p = len(pair_ai)
    a_refs = refs[:n_a]
    b_refs = refs[n_a:n_a + n_p]
    e_refs = refs[n_a + n_p:n_a + n_p + n_extra]
    o_ref = refs[n_a + n_p + n_extra]
    dots = [jnp.dot(a_refs[ai][...], b_refs[p][...].astype(BF16),
                    preferred_element_type=F32) for p, ai in enumerate(pair_ai)]
    extras = [r[...] for r in e_refs]
    o_ref[...] = epilogue(dots, extras, pl.program_id(1)).astype(o_ref.dtype)


def _matmul_fullk(a_list, pairs, extras, epilogue, *, m_rows, n_cols, out_dtype,
                  tm, tn, name, a_buffers=2):
    assert m_rows % tm == 0 and n_cols % tn == 0
    a_mode = {} if a_buffers == 2 else {"pipeline_mode": pl.Buffered(a_buffers)}
    in_specs = [pl.BlockSpec((tm, a.shape[1]), lambda i, j: (i, 0), **a_mode)
                for a in a_list]
    for ai, b, col_off in pairs:
        assert b.shape[0] == a_list[ai].shape[1]
        in_specs.append(pl.BlockSpec((b.shape[0], tn),
                                     lambda i, j, col_off=col_off: (0, j + col_off)))
    for _ in extras:
        in_specs.append(pl.BlockSpec((tm, tn), lambda i, j: (i, j)))
    body = functools.partial(_mm_fullk_body, tuple(ai for ai, _, _ in pairs),
                             len(a_list), len(extras), epilogue)
    return pl.pallas_call(
        body,
        grid=(m_rows // tm, n_cols // tn),
        in_specs=in_specs,
        out_specs=pl.BlockSpec((tm, tn), lambda i, j: (i, j)),
        out_shape=jax.ShapeDtypeStruct((m_rows, n_cols), out_dtype),
        compiler_params=pltpu.CompilerParams(
            dimension_semantics=("parallel", "arbitrary"),
            vmem_limit_bytes=VMEM_LIMIT),
        name=name,
    )(*a_list, *[b for _, b, _ in pairs], *extras)


def _mm_nt_body(shift, epilogue, tab_ref, a_ref, bm_ref, *rest):
    del tab_ref
    b = bm_ref[...]
    if shift:
        bn_ref, o_ref = rest
        b = jnp.concatenate([b[shift:], bn_ref[...]], axis=0)
    else:
        (o_ref,) = rest
    d = lax.dot_general(a_ref[...], b.astype(BF16), (((1,), (1,)), ((), ())),
                        preferred_element_type=F32)
    o_ref[...] = epilogue(d, pl.program_id(1)).astype(o_ref.dtype)


def _matmul_nt(a, bt, tile_starts, shift, epilogue, *, m_rows, out_dtype, tm, tn, name):
    assert m_rows % tm == 0 and all(s % tn == 0 for s in tile_starts)
    assert shift % 8 == 0 and (shift == 0 or tn % shift == 0)
    k_total = a.shape[1]
    assert bt.shape[1] == k_total
    tab = jnp.asarray([s // tn for s in tile_starts], jnp.int32)
    in_specs = [pl.BlockSpec((tm, k_total), lambda i, j, tab: (i, 0)),
                pl.BlockSpec((tn, k_total), lambda i, j, tab: (tab[j], 0))]
    operands = [tab, a, bt]
    if shift:
        per = tn // shift
        in_specs.append(pl.BlockSpec((shift, k_total),
                                     lambda i, j, tab: ((tab[j] + 1) * per, 0)))
        operands.append(bt)
    grid_spec = pltpu.PrefetchScalarGridSpec(
        num_scalar_prefetch=1,
        grid=(m_rows // tm, len(tile_starts)),
        in_specs=in_specs,
        out_specs=pl.BlockSpec((tm, tn), lambda i, j, tab: (i, j)),
    )
    return pl.pallas_call(
        functools.partial(_mm_nt_body, shift, epilogue),
        grid_spec=grid_spec,
        out_shape=jax.ShapeDtypeStruct((m_rows, tn * len(tile_starts)), out_dtype),
        compiler_params=pltpu.CompilerParams(
            dimension_semantics=("parallel", "arbitrary"),
            vmem_limit_bytes=VMEM_LIMIT),
        name=name,
    )(*operands)


def _mm_kacc_body(tk, k_total, scale, has_tail, a_ref, b_ref, r_ref, *rest):
    if has_tail:
        tail_ref, o_ref = rest
    else:
        (o_ref,) = rest
    k = pl.program_id(2)
    last_row_block = pl.program_id(0) == pl.num_programs(0) - 1

    def residual():
        r = r_ref[...]
        if has_tail:
            tm, t_rows = r.shape[0], tail_ref.shape[0]
            tail = jnp.concatenate(
                [jnp.zeros((tm - t_rows, r.shape[1]), r.dtype), tail_ref[...]], axis=0)
            row = lax.broadcasted_iota(jnp.int32, r.shape, 0)
            in_tail = jnp.logical_and(last_row_block, row >= tm - t_rows)
            r = jnp.where(in_tail, tail, r)
        return r

    def partial_product():
        a = a_ref[...]
        b = b_ref[...]
        if k_total % tk:
            limit = k_total - k * tk
            a = jnp.where(lax.broadcasted_iota(jnp.int32, a.shape, 1) < limit, a,
                          jnp.zeros_like(a))
            b = jnp.where(lax.broadcasted_iota(jnp.int32, b.shape, 0) < limit, b,
                          jnp.zeros_like(b))
        return scale * jnp.dot(a, b.astype(BF16), preferred_element_type=F32)

    @pl.when(k == 0)
    def _():
        o_ref[...] = residual() + partial_product()

    @pl.when(k > 0)
    def _():
        o_ref[...] += partial_product()


def _matmul_kacc(a, b, res, scale, *, m_rows, tm, tn, tk, name, res_tail=None):
    k_total, n_cols = b.shape
    assert a.shape[1] == k_total and m_rows % tm == 0 and n_cols % tn == 0
    in_specs = [pl.BlockSpec((tm, tk), lambda i, j, k: (i, k)),
                pl.BlockSpec((tk, tn), lambda i, j, k: (k, j)),
                pl.BlockSpec((tm, tn), lambda i, j, k: (i, j))]
    operands = [a, b, res]
    if res_tail is not None:
        assert res.shape[0] + res_tail.shape[0] == m_rows
        in_specs.append(pl.BlockSpec((res_tail.shape[0], tn), lambda i, j, k: (0, j)))
        operands.append(res_tail)
    return pl.pallas_call(
        functools.partial(_mm_kacc_body, tk, k_total, scale, res_tail is not None),
        grid=(m_rows // tm, n_cols // tn, pl.cdiv(k_total, tk)),
        in_specs=in_specs,
        out_specs=pl.BlockSpec((tm, tn), lambda i, j, k: (i, j)),
        out_shape=jax.ShapeDtypeStruct((m_rows, n_cols), F32),
        compiler_params=pltpu.CompilerParams(
            dimension_semantics=("parallel", "parallel", "arbitrary"),
            vmem_limit_bytes=VMEM_LIMIT),
        name=name,
    )(*operands)


def _swiglu_ffn(n, h_res, w1, w3, w2, *, m_rows, tm_up, tm_down, res_tail=None):
    def up_epilogue(dots, extras, j):
        a1, a3 = dots
        return a1 * _sigmoid(a1) * a3

    u = _matmul_fullk([n], [(0, w1, 0), (0, w3, 0)], [], up_epilogue,
                      m_rows=m_rows, n_cols=w1.shape[1], out_dtype=BF16,
                      tm=tm_up, tn=256, name="ffn_up")
    return _matmul_kacc(u, w2, h_res, 0.5, m_rows=m_rows, tm=tm_down, tn=1024, tk=1024,
                        name="ffn_down", res_tail=res_tail)


def _small_proj_body(f_off, ab_off, n_ref, wf_ref, fb_ref, wab_ref, kaug_ref, ab_ref,
                     carry_ref):
    t = pl.program_id(0)
    is_meta = t == 0
    nt_dims = (((1,), (1,)), ((), ()))

    @pl.when(is_meta)
    def _():
        carry_ref[...] = jnp.zeros_like(carry_ref)

    n = n_ref[...]
    ft = lax.dot_general(n, wf_ref[...].astype(BF16), nt_dims,
                         preferred_element_type=F32)
    row = lax.broadcasted_iota(jnp.int32, ft.shape, 0)
    col = lax.broadcasted_iota(jnp.int32, ft.shape, 1)
    ft = jnp.where(jnp.logical_and(col >= f_off, col < f_off + FOX_HEADS), ft, 0.0)
    shifts = [(p * FOX_HEADS - f_off) % LANE for p in range(3)]
    ft = sum(pltpu.roll(ft, s, 1) if s else ft for s in shifts)
    logf = _log_sigmoid(ft + fb_ref[...])
    valid = jnp.logical_or(jnp.logical_not(is_meta), row < N_META)
    logf = jnp.where(valid, logf, 0.0)
    tsz = logf.shape[0]
    lower = (lax.broadcasted_iota(jnp.int32, (tsz, tsz), 1)
             <= lax.broadcasted_iota(jnp.int32, (tsz, tsz), 0)).astype(F32)
    c = carry_ref[0:1, :] + jnp.dot(lower, logf, precision=lax.Precision.HIGHEST,
                                    preferred_element_type=F32)
    carry_ref[...] = carry_ref[...] + jnp.sum(logf, axis=0, keepdims=True)

    bias = -LOG2E * c
    hi = bias.astype(BF16).astype(F32)
    mid = (bias - hi).astype(BF16).astype(F32)
    lo = (bias - hi - mid).astype(BF16).astype(F32)
    pieces = jnp.where(col < FOX_HEADS, hi,
                       jnp.where(col < 2 * FOX_HEADS, mid,
                                 jnp.where(col < 3 * FOX_HEADS, lo, 0.0)))
    removed = jnp.where(col < FOX_HEADS, NEG_BIG, 0.0)
    kaug_ref[...] = jnp.where(valid, pieces, removed).astype(BF16)
    wrow = lax.broadcasted_iota(jnp.int32, wab_ref.shape, 0)
    wab = jnp.where(jnp.logical_and(wrow >= ab_off, wrow < ab_off + GLA_GATE_RANK),
                    wab_ref[...], 0.0)
    ab_ref[...] = lax.dot_general(n, wab.astype(BF16), nt_dims,
                                  preferred_element_type=F32)


def _small_proj(n, wt, f_bias, *, seq, f_col, ab_col):
    lp, d = n.shape
    heads = f_bias.shape[0]
    assert heads == FOX_HEADS and 3 * heads <= LANE
    assert f_col % LANE + heads <= LANE and ab_col % LANE + GLA_GATE_RANK <= LANE
    tsz = META_BLOCK
    n_x = seq // tsz
    fb3 = jnp.concatenate([f_bias, f_bias, f_bias,
                           jnp.zeros((LANE - 3 * heads,), f_bias.dtype)])

    def row_block(t):
        return jnp.where(t == 0, n_x, t - 1)

    return pl.pallas_call(
        functools.partial(_small_proj_body, f_col % LANE, ab_col % LANE),
        grid=(n_x + 1,),
        in_specs=[pl.BlockSpec((tsz, d), lambda t: (row_block(t), 0)),
                  pl.BlockSpec((LANE, d), lambda t: (f_col // LANE, 0)),
                  pl.BlockSpec((1, LANE), lambda t: (0, 0)),
                  pl.BlockSpec((LANE, d), lambda t: (ab_col // LANE, 0))],
        out_specs=[pl.BlockSpec((tsz, LANE), lambda t: (row_block(t), 0)),
                   pl.BlockSpec((tsz, LANE), lambda t: (row_block(t), 0))],
        out_shape=[jax.ShapeDtypeStruct((lp, LANE), BF16),
                   jax.ShapeDtypeStruct((lp, LANE), F32)],
        scratch_shapes=[pltpu.VMEM((8, LANE), F32)],
        compiler_params=pltpu.CompilerParams(
            dimension_semantics=("arbitrary",), vmem_limit_bytes=VMEM_LIMIT),
        name="small_proj",
    )(n, wt, fb3.reshape(1, LANE), wt)


def _fox_body(tq, rq, i_tab, j_tab, q_ref, km_ref, vm_ref, am_ref, kx_ref, vx_ref,
              ax_ref, o_ref, m_ref, acc_ref):
    h = pl.program_id(0)
    t = pl.program_id(1)
    i = i_tab[t]
    j = j_tab[t]
    dh = q_ref.shape[1]
    lane = lax.broadcasted_iota(jnp.int32, (rq, LANE), 1)
    pick = jnp.logical_or(lane == h, jnp.logical_or(lane == h + FOX_HEADS,
                                                    lane == h + 2 * FOX_HEADS))
    q_ones = jnp.where(pick, 1.0, 0.0).astype(BF16)

    def online_update(k_ref, v_ref, a_ref, diagonal):
        n_k = k_ref.shape[0]
        keys = jnp.concatenate([k_ref[...], a_ref[...]], axis=1)
        vals = jnp.concatenate([v_ref[...], jnp.ones((n_k, LANE), BF16)], axis=1)
        for r in range(tq // rq):
            rows = pl.ds(r * rq, rq)
            n_c = (r + 1) * rq if diagonal else n_k
            q = jnp.concatenate([q_ref[rows, :], q_ones], axis=1)
            s = lax.dot_general(q, keys[:n_c], (((1,), (1,)), ((), ())),
                                preferred_element_type=F32)
            if diagonal:
                tri = (lax.broadcasted_iota(jnp.int32, (rq, rq), 1)
                       <= lax.broadcasted_iota(jnp.int32, (rq, rq), 0))
                s_last = jnp.where(tri, s[:, n_c - rq:], NEG_BIG)
                s = s_last if r == 0 else jnp.concatenate(
                    [s[:, :n_c - rq], s_last], axis=1)
            m_prev = m_ref[rows, :]
            m_next = jnp.maximum(m_prev, jnp.max(s, axis=1, keepdims=True))
            p = jnp.exp2(s - jnp.tile(m_next, (1, n_c // LANE)))
            alpha = jnp.exp2(m_prev - m_next)
            m_ref[rows, :] = m_next
            acc_ref[rows, :] = (jnp.tile(alpha, (1, 2)) * acc_ref[rows, :]
                                + jnp.dot(p.astype(BF16), vals[:n_c],
                                          preferred_element_type=F32))

    @pl.when(j == 0)
    def _():
        m_ref[...] = jnp.full_like(m_ref, NEG_BIG)
        acc_ref[...] = jnp.zeros_like(acc_ref)
        online_update(km_ref, vm_ref, am_ref, False)

    @pl.when(jnp.logical_and(j >= 1, j - 1 < i))
    def _():
        online_update(kx_ref, vx_ref, ax_ref, False)

    @pl.when(j - 1 == i)
    def _():
        online_update(kx_ref, vx_ref, ax_ref, True)
        acc = acc_ref[...]
        o_ref[...] = (acc[:, :dh] / acc[:, dh:]).astype(o_ref.dtype)


def _fox_attention(qkv, kaug, *, seq, tq, rq):
    heads = FOX_HEADS
    dh = FOX_HEAD_DIM
    assert dh == LANE and seq % tq == 0 and tq % rq == 0 and rq % LANE == 0
    n_q = seq // tq
    meta_blk = seq // META_BLOCK
    steps = [(i, j) for i in range(n_q) for j in range(i + 2)]
    i_tab = jnp.asarray([s[0] for s in steps], jnp.int32)
    j_tab = jnp.asarray([s[1] for s in steps], jnp.int32)

    def kv_block(t, j_tab):
        return jnp.maximum(j_tab[t] - 1, 0)

    grid_spec = pltpu.PrefetchScalarGridSpec(
        num_scalar_prefetch=2,
        grid=(heads, len(steps)),
        in_specs=[
            pl.BlockSpec((tq, dh), lambda h, t, it, jt: (it[t], h)),
            pl.BlockSpec((META_BLOCK, dh), lambda h, t, it, jt: (meta_blk, heads + h)),
            pl.BlockSpec((META_BLOCK, dh),
                         lambda h, t, it, jt: (meta_blk, 2 * heads + h)),
            pl.BlockSpec((META_BLOCK, LANE), lambda h, t, it, jt: (meta_blk, 0)),
            pl.BlockSpec((tq, dh), lambda h, t, it, jt: (kv_block(t, jt), heads + h)),
            pl.BlockSpec((tq, dh),
                         lambda h, t, it, jt: (kv_block(t, jt), 2 * heads + h)),
            pl.BlockSpec((tq, LANE), lambda h, t, it, jt: (kv_block(t, jt), 0)),
        ],
        out_specs=pl.BlockSpec((tq, dh), lambda h, t, it, jt: (it[t], h)),
        scratch_shapes=[pltpu.VMEM((tq, LANE), F32), pltpu.VMEM((tq, 2 * dh), F32)],
    )
    return pl.pallas_call(
        functools.partial(_fox_body, tq, rq),
        grid_spec=grid_spec,
        out_shape=jax.ShapeDtypeStruct((seq, heads * dh), BF16),
        compiler_params=pltpu.CompilerParams(
            dimension_semantics=("parallel", "arbitrary"),
            vmem_limit_bytes=VMEM_LIMIT),
        name="fox_attention",
    )(i_tab, j_tab, qkv, qkv, qkv, kaug, qkv, qkv, kaug)


def _gla_body(qv_ref, kr_ref, ab_ref, w2_ref, b_ref, g_ref, o_ref, state_ref):
    t = pl.program_id(0)
    is_meta = t == 0

    @pl.when(is_meta)
    def _():
        state_ref[...] = jnp.zeros_like(state_ref)

    z = jnp.dot(ab_ref[...].astype(BF16), w2_ref[...].astype(BF16),
                preferred_element_type=F32) + b_ref[...]
    la = _log_sigmoid(z) / GLA_GATE_TAU
    row = lax.broadcasted_iota(jnp.int32, la.shape, 0)
    la = jnp.where(jnp.logical_and(is_meta, row >= N_META), 0.0, la)
    c = la.shape[0]
    lower = (lax.broadcasted_iota(jnp.int32, (c, c), 1)
             <= lax.broadcasted_iota(jnp.int32, (c, c), 0)).astype(F32)
    cum = jnp.dot(lower, la, precision=lax.Precision.HIGHEST,
                  preferred_element_type=F32)
    total = cum[c - 1:c, :]
    k_dec = (kr_ref[:, :GLA_KW] * jnp.exp(total - cum)).astype(BF16)
    decay = jnp.exp(total)

    for h in range(GLA_HEADS):
        ks = slice(h * GLA_DK, (h + 1) * GLA_DK)
        vs = slice(GLA_KW + h * GLA_DV, GLA_KW + (h + 1) * GLA_DV)
        kv = lax.dot_general(qv_ref[:, vs], k_dec[:, ks], (((0,), (0,)), ((), ())),
                             preferred_element_type=F32)
        s_new = state_ref[h] * decay[:, ks] + kv
        state_ref[h] = s_new
        o = lax.dot_general(qv_ref[:, ks], s_new.astype(BF16),
                            (((1,), (1,)), ((), ())),
                            preferred_element_type=F32)
        ms = jnp.mean(o * o, axis=-1, keepdims=True)
        r = kr_ref[:, vs]
        o_ref[:, h * GLA_DV:(h + 1) * GLA_DV] = (
            o * lax.rsqrt(ms + EPS) * g_ref[...] * (r * _sigmoid(r))
        ).astype(o_ref.dtype)


def _gla(qv, kr, ab, w2, b, g, *, seq, ab_off):
    n_x = seq // CHUNK
    w2 = jnp.zeros((LANE, GLA_KW), w2.dtype).at[ab_off:ab_off + GLA_GATE_RANK].set(w2)

    def row_block(t):
        return jnp.where(t == 0, n_x, t - 1)

    width = GLA_KW + GLA_VW
    return pl.pallas_call(
        _gla_body,
        grid=(n_x + 1,),
        in_specs=[pl.BlockSpec((CHUNK, width), lambda t: (row_block(t), 0)),
                  pl.BlockSpec((CHUNK, width), lambda t: (row_block(t), 0)),
                  pl.BlockSpec((CHUNK, LANE), lambda t: (row_block(t), 0)),
                  pl.BlockSpec((LANE, GLA_KW), lambda t: (0, 0)),
                  pl.BlockSpec((1, GLA_KW), lambda t: (0, 0)),
                  pl.BlockSpec((1, GLA_DV), lambda t: (0, 0))],
        out_specs=pl.BlockSpec((CHUNK, GLA_VW), lambda t: (jnp.maximum(t - 1, 0), 0)),
        out_shape=jax.ShapeDtypeStruct((seq, GLA_VW), BF16),
        scratch_shapes=[pltpu.VMEM((GLA_HEADS, GLA_DV, GLA_DK), F32)],
        compiler_params=pltpu.CompilerParams(
            dimension_semantics=("arbitrary",), vmem_limit_bytes=VMEM_LIMIT),
        name="gla",
    )(qv, kr, ab, w2, b.reshape(1, GLA_KW), g.reshape(1, GLA_DV))


def kernel(x, meta_tokens, norm1_g, ffn1_w1, ffn1_w3, ffn1_w2, norm_mix_g, w_in,
           fox_f_bias, gla_alpha_w2, gla_alpha_b, gla_norm_g, w_gate, w_proj_fox,
           w_proj_gla, w_out, norm2_g, ffn2_w1, ffn2_w3, ffn2_w2, norm_final_g):
    batch, seq, d = x.shape
    assert batch == 1 and seq % 1024 == 0
    depth = norm1_g.shape[0]
    lp = seq + META_BLOCK
    assert lp % 8 == 0 and (lp // 8) % 16 == 0
    tm_all = lp // 8
    tm_x = seq // 8

    norm_tile = 2 * META_BLOCK
    meta_pad = jnp.concatenate(
        [meta_tokens.astype(x.dtype), jnp.zeros((norm_tile - N_META, d), x.dtype)], axis=0)
    x2 = x[0]

    o_fa = FOX_WIDTH * 3
    o_qb = o_fa + FOX_HEADS
    o_kb = o_qb + GLA_KW
    o_vb = o_kb + GLA_KW
    o_rb = o_vb + GLA_VW
    o_ab = o_rb + GLA_VW
    fox_scale = FOX_HEAD_DIM ** -0.5 * LOG2E
    gla_scale = GLA_DK ** -0.5

    assert depth == 1
    for li in range(depth):
        n1 = _rmsnorm_frames_meta(x2, meta_pad, norm1_g[li], lp=lp)
        h = _swiglu_ffn(n1, x2, ffn1_w1[li], ffn1_w3[li], ffn1_w2[li], m_rows=lp,
                        tm_up=tm_all, tm_down=2 * tm_all, res_tail=meta_pad[:META_BLOCK])

        n = _rmsnorm(h, norm_mix_g[li], lp, BF16)
        wt = jnp.transpose(w_in[li])

        tn_p = 512
        shift = o_qb % tn_p

        def tiles(start, width):
            assert start % tn_p in (0, shift) and width % tn_p == 0
            return [start - start % tn_p + r for r in range(0, width, tn_p)]

        def fox_epilogue(dot, j):
            return dot * jnp.where(j < FOX_WIDTH // tn_p, fox_scale, 1.0)

        qkv_a = _matmul_nt(n, wt, tiles(0, 3 * FOX_WIDTH), 0, fox_epilogue, m_rows=lp,
                           out_dtype=BF16, tm=tm_all, tn=tn_p, name="proj_fox")

        def gla_q_epilogue(dot, j):
            return dot * jnp.where(j < GLA_KW // tn_p, gla_scale, 1.0)

        qv_b = _matmul_nt(n, wt, tiles(o_qb, GLA_KW) + tiles(o_vb, GLA_VW), shift,
                          gla_q_epilogue, m_rows=lp, out_dtype=BF16,
                          tm=tm_all, tn=tn_p, name="proj_gla_qv")
        kr_b = _matmul_nt(n, wt, tiles(o_kb, GLA_KW) + tiles(o_rb, GLA_VW), shift,
                          lambda dot, j: dot, m_rows=lp, out_dtype=F32,
                          tm=tm_all, tn=tn_p, name="proj_gla_kr")

        kaug, ab = _small_proj(n, wt, fox_f_bias[li], seq=seq, f_col=o_fa, ab_col=o_ab)
        o_a = _fox_attention(qkv_a, kaug, seq=seq, tq=2048, rq=256)
        o_b = _gla(qv_b, kr_b, ab, gla_alpha_w2[li], gla_alpha_b[li], gla_norm_g[li],
                   seq=seq, ab_off=o_ab % LANE)

        wg = w_gate[li]

        def mix_epilogue(accs, extras, j):
            ga, gb, pa, pb = accs
            return _sigmoid(ga) * pa + _sigmoid(gb) * pb

        tn_m = 256
        y = _matmul_fullk(
            [n, o_a, o_b],
            [(0, wg, 0), (0, wg, d // tn_m), (1, w_proj_fox[li], 0),
             (2, w_proj_gla[li], 0)],
            [], mix_epilogue, m_rows=seq, n_cols=d, out_dtype=BF16,
            tm=tm_x, tn=tn_m, name="mix", a_buffers=1)

        h = _matmul_fullk([y], [(0, w_out[li], 0)], [h],
                          lambda a, e, j: e[0] + a[0], m_rows=seq, n_cols=d,
                          out_dtype=F32, tm=tm_x, tn=512, name="out_proj")

        n2 = _rmsnorm(h, norm2_g[li], seq, BF16)
        h = _swiglu_ffn(n2, h, ffn2_w1[li], ffn2_w3[li], ffn2_w2[li], m_rows=seq,
                        tm_up=tm_x, tm_down=2 * tm_x)

    out = _rmsnorm(h, norm_final_g, seq, F32)
    return out.reshape(batch, seq, d)
```

```python
import functools

import jax
import jax.numpy as jnp
from jax import lax
from jax.experimental import pallas as pl
from jax.experimental.pallas import tpu as pltpu

F32 = jnp.float32
BF16 = jnp.bfloat16

EPS = 1e-6
N_META = 16
CHUNK = 64
FOX_HEADS = 16
FOX_HEAD_DIM = 128
FOX_WIDTH = FOX_HEADS * FOX_HEAD_DIM
GLA_HEADS = 4
GLA_DK = 256
GLA_DV = 512
GLA_KW = GLA_HEADS * GLA_DK
GLA_VW = GLA_HEADS * GLA_DV
GLA_GATE_RANK = 16
GLA_GATE_TAU = 16.0

LANE = 128
META_BLOCK = LANE
NEG_BIG = -1e30
LOG2E = 1.4426950408889634
VMEM_LIMIT = 60 * 1024 * 1024


def _log_sigmoid(z):
    return jnp.minimum(z, 0.0) - jnp.log(1.0 + jnp.exp(-jnp.abs(z)))


def _sigmoid(z):
    return 1.0 / (1.0 + jnp.exp(-z))


def _rmsnorm_body(x_ref, g_ref, o_ref):
    x = x_ref[...]
    ms = jnp.mean(x * x, axis=-1, keepdims=True)
    o_ref[...] = (x * lax.rsqrt(ms + EPS) * g_ref[...]).astype(o_ref.dtype)


def _rmsnorm(x, g, rows, out_dtype):
    d = x.shape[1]
    tile = max(t for t in range(16, 513, 16) if rows % t == 0)
    return pl.pallas_call(
        _rmsnorm_body,
        grid=(rows // tile,),
        in_specs=[pl.BlockSpec((tile, d), lambda i: (i, 0)),
                  pl.BlockSpec((1, d), lambda i: (0, 0))],
        out_specs=pl.BlockSpec((tile, d), lambda i: (i, 0)),
        out_shape=jax.ShapeDtypeStruct((rows, d), out_dtype),
        compiler_params=pltpu.CompilerParams(
            dimension_semantics=("parallel",), vmem_limit_bytes=VMEM_LIMIT),
        name="rmsnorm",
    )(x, g.reshape(1, d))


def _rmsnorm_frames_meta_body(n_x, x_ref, mp_ref, g_ref, o_ref):
    src = jnp.where(pl.program_id(0) < n_x, x_ref[...], mp_ref[...])
    ms = jnp.mean(src * src, axis=-1, keepdims=True)
    o_ref[...] = (src * lax.rsqrt(ms + EPS) * g_ref[...]).astype(o_ref.dtype)


def _rmsnorm_frames_meta(x, meta_pad, g, *, lp):
    seq, d = x.shape
    tile = meta_pad.shape[0]
    assert seq % tile == 0 and seq < lp <= seq + tile
    n_x = seq // tile
    return pl.pallas_call(
        functools.partial(_rmsnorm_frames_meta_body, n_x),
        grid=(n_x + 1,),
        in_specs=[pl.BlockSpec((tile, d), lambda i: (jnp.minimum(i, n_x - 1), 0)),
                  pl.BlockSpec((tile, d), lambda i: (0, 0)),
                  pl.BlockSpec((1, d), lambda i: (0, 0))],
        out_specs=pl.BlockSpec((tile, d), lambda i: (i, 0)),
        out_shape=jax.ShapeDtypeStruct((lp, d), BF16),
        compiler_params=pltpu.CompilerParams(
            dimension_semantics=("parallel",), vmem_limit_bytes=VMEM_LIMIT),
        name="rmsnorm_in",
    )(x, meta_pad, g.reshape(1, d))


def _mm_fullk_body(pair_ai, n_a, n_extra, epilogue, *refs):
    n_p = len(pair_ai)
    a_refs = refs[:n_a]
    b_refs = refs[n_a:n_a + n_p]
    e_refs = refs[n_a + n_p:n_a + n_p + n_extra]
    o_ref = refs[n_a + n_p + n_extra]
    dots = [jnp.dot(a_refs[ai][...], b_refs[p][...].astype(BF16),
                    preferred_element_type=F32) for p, ai in enumerate(pair_ai)]
    extras = [r[...] for r in e_refs]
    o_ref[...] = epilogue(dots, extras, pl.program_id(1)).astype(o_ref.dtype)


def _matmul_fullk(a_list, pairs, extras, epilogue, *, m_rows, n_cols, out_dtype,
                  tm, tn, name, a_buffers=2):
    assert m_rows % tm == 0 and n_cols % tn == 0
    a_mode = {} if a_buffers == 2 else {"pipeline_mode": pl.Buffered(a_buffers)}
    in_specs = [pl.BlockSpec((tm, a.shape[1]), lambda i, j: (i, 0), **a_mode)
                for a in a_list]
    for ai, b, col_off in pairs:
        assert b.shape[0] == a_list[ai].shape[1]
        in_specs.append(pl.BlockSpec((b.shape[0], tn),
                                     lambda i, j, col_off=col_off: (0, j + col_off)))
    for _ in extras:
        in_specs.append(pl.BlockSpec((tm, tn), lambda i, j: (i, j)))
    body = functools.partial(_mm_fullk_body, tuple(ai for ai, _, _ in pairs),
                             len(a_list), len(extras), epilogue)
    return pl.pallas_call(
        body,
        grid=(m_rows // tm, n_cols // tn),
        in_specs=in_specs,
        out_specs=pl.BlockSpec((tm, tn), lambda i, j: (i, j)),
        out_shape=jax.ShapeDtypeStruct((m_rows, n_cols), out_dtype),
        compiler_params=pltpu.CompilerParams(
            dimension_semantics=("parallel", "arbitrary"),
            vmem_limit_bytes=VMEM_LIMIT),
        name=name,
    )(*a_list, *[b for _, b, _ in pairs], *extras)


def _mm_nt_body(shift, epilogue, tab_ref, a_ref, bm_ref, *rest):
    del tab_ref
    b = bm_ref[...]
    if shift:
        bn_ref, o_ref = rest
        b = jnp.concatenate([b[shift:], bn_ref[...]], axis=0)
    else:
        (o_ref,) = rest
    d = lax.dot_general(a_ref[...], b.astype(BF16), (((1,), (1,)), ((), ())),
                        preferred_element_type=F32)
    o_ref[...] = epilogue(d, pl.program_id(1)).astype(o_ref.dtype)


def _matmul_nt(a, bt, tile_starts, shift, epilogue, *, m_rows, out_dtype, tm, tn, name):
    assert m_rows % tm == 0 and all(s % tn == 0 for s in tile_starts)
    assert shift % 8 == 0 and (shift == 0 or tn % shift == 0)
    k_total = a.shape[1]
    assert bt.shape[1] == k_total
    tab = jnp.asarray([s // tn for s in tile_starts], jnp.int32)
    in_specs = [pl.BlockSpec((tm, k_total), lambda i, j, tab: (i, 0)),
                pl.BlockSpec((tn, k_total), lambda i, j, tab: (tab[j], 0))]
    operands = [tab, a, bt]
    if shift:
        per = tn // shift
        in_specs.append(pl.BlockSpec((shift, k_total),
                                     lambda i, j, tab: ((tab[j] + 1) * per, 0)))
        operands.append(bt)
    grid_spec = pltpu.PrefetchScalarGridSpec(
        num_scalar_prefetch=1,
        grid=(m_rows // tm, len(tile_starts)),
        in_specs=in_specs,
        out_specs=pl.BlockSpec((tm, tn), lambda i, j, tab: (i, j)),
    )
    return pl.pallas_call(
        functools.partial(_mm_nt_body, shift, epilogue),
        grid_spec=grid_spec,
        out_shape=jax.ShapeDtypeStruct((m_rows, tn * len(tile_starts)), out_dtype),
        compiler_params=pltpu.CompilerParams(
            dimension_semantics=("parallel", "arbitrary"),
            vmem_limit_bytes=VMEM_LIMIT),
        name=name,
    )(*operands)


def _mm_kacc_body(tk, k_total, scale, has_tail, a_ref, b_ref, r_ref, *rest):
    if has_tail:
        tail_ref, o_ref = rest
    else:
        (o_ref,) = rest
    k = pl.program_id(2)
    last_row_block = pl.program_id(0) == pl.num_programs(0) - 1

    def residual():
        r = r_ref[...]
        if has_tail:
            tm, t_rows = r.shape[0], tail_ref.shape[0]
            tail = jnp.concatenate(
                [jnp.zeros((tm - t_rows, r.shape[1]), r.dtype), tail_ref[...]], axis=0)
            row = lax.broadcasted_iota(jnp.int32, r.shape, 0)
            in_tail = jnp.logical_and(last_row_block, row >= tm - t_rows)
            r = jnp.where(in_tail, tail, r)
        return r

    def partial_product():
        a = a_ref[...]
        b = b_ref[...]
        if k_total % tk:
            limit = k_total - k * tk
            a = jnp.where(lax.broadcasted_iota(jnp.int32, a.shape, 1) < limit, a,
                          jnp.zeros_like(a))
            b = jnp.where(lax.broadcasted_iota(jnp.int32, b.shape, 0) < limit, b,
                          jnp.zeros_like(b))
        return scale * jnp.dot(a, b.astype(BF16), preferred_element_type=F32)

    @pl.when(k == 0)
    def _():
        o_ref[...] = residual() + partial_product()

    @pl.when(k > 0)
    def _():
        o_ref[...] += partial_product()


def _matmul_kacc(a, b, res, scale, *, m_rows, tm, tn, tk, name, res_tail=None):
    k_total, n_cols = b.shape
    assert a.shape[1] == k_total and m_rows % tm == 0 and n_cols % tn == 0
    in_specs = [pl.BlockSpec((tm, tk), lambda i, j, k: (i, k)),
                pl.BlockSpec((tk, tn), lambda i, j, k: (k, j)),
                pl.BlockSpec((tm, tn), lambda i, j, k: (i, j))]
    operands = [a, b, res]
    if res_tail is not None:
        assert res.shape[0] + res_tail.shape[0] == m_rows
        in_specs.append(pl.BlockSpec((res_tail.shape[0], tn), lambda i, j, k: (0, j)))
        operands.append(res_tail)
    return pl.pallas_call(
        functools.partial(_mm_kacc_body, tk, k_total, scale, res_tail is not None),
        grid=(m_rows // tm, n_cols // tn, pl.cdiv(k_total, tk)),
        in_specs=in_specs,
        out_specs=pl.BlockSpec((tm, tn), lambda i, j, k: (i, j)),
        out_shape=jax.ShapeDtypeStruct((m_rows, n_cols), F32),
        compiler_params=pltpu.CompilerParams(
            dimension_semantics=("parallel", "parallel", "arbitrary"),
            vmem_limit_bytes=VMEM_LIMIT),
        name=name,
    )(*operands)


def _swiglu_ffn(n, h_res, w1, w3, w2, *, m_rows, tm_up, tm_down, res_tail=None):
    def up_epilogue(dots, extras, j):
        a1, a3 = dots
        return a1 * _sigmoid(a1) * a3

    u = _matmul_fullk([n], [(0, w1, 0), (0, w3, 0)], [], up_epilogue,
                      m_rows=m_rows, n_cols=w1.shape[1], out_dtype=BF16,
                      tm=tm_up, tn=256, name="ffn_up", a_buffers=1)
    return _matmul_kacc(u, w2, h_res, 0.5, m_rows=m_rows, tm=tm_down, tn=1024, tk=1024,
                        name="ffn_down", res_tail=res_tail)


def _small_proj_body(f_off, ab_off, n_ref, wf_ref, fb_ref, wab_ref, kaug_ref, ab_ref,
                     carry_ref):
    t = pl.program_id(0)
    is_meta = t == 0
    nt_dims = (((1,), (1,)), ((), ()))

    @pl.when(is_meta)
    def _():
        carry_ref[...] = jnp.zeros_like(carry_ref)

    n = n_ref[...]
    ft = lax.dot_general(n, wf_ref[...].astype(BF16), nt_dims,
                         preferred_element_type=F32)
    row = lax.broadcasted_iota(jnp.int32, ft.shape, 0)
    col = lax.broadcasted_iota(jnp.int32, ft.shape, 1)
    ft = jnp.where(jnp.logical_and(col >= f_off, col < f_off + FOX_HEADS), ft, 0.0)
    shifts = [(p * FOX_HEADS - f_off) % LANE for p in range(3)]
    ft = sum(pltpu.roll(ft, s, 1) if s else ft for s in shifts)
    logf = _log_sigmoid(ft + fb_ref[...])
    valid = jnp.logical_or(jnp.logical_not(is_meta), row < N_META)
    logf = jnp.where(valid, logf, 0.0)
    tsz = logf.shape[0]
    lower = (lax.broadcasted_iota(jnp.int32, (tsz, tsz), 1)
             <= lax.broadcasted_iota(jnp.int32, (tsz, tsz), 0)).astype(F32)
    c = carry_ref[0:1, :] + jnp.dot(lower, logf, precision=lax.Precision.HIGHEST,
                                    preferred_element_type=F32)
    carry_ref[...] = carry_ref[...] + jnp.sum(logf, axis=0, keepdims=True)

    bias = -LOG2E * c
    hi = bias.astype(BF16).astype(F32)
    mid = (bias - hi).astype(BF16).astype(F32)
    lo = (bias - hi - mid).astype(BF16).astype(F32)
    pieces = jnp.where(col < FOX_HEADS, hi,
                       jnp.where(col < 2 * FOX_HEADS, mid,
                                 jnp.where(col < 3 * FOX_HEADS, lo, 0.0)))
    removed = jnp.where(col < FOX_HEADS, NEG_BIG, 0.0)
    kaug_ref[...] = jnp.where(valid, pieces, removed).astype(BF16)
    wrow = lax.broadcasted_iota(jnp.int32, wab_ref.shape, 0)
    wab = jnp.where(jnp.logical_and(wrow >= ab_off, wrow < ab_off + GLA_GATE_RANK),
                    wab_ref[...], 0.0)
    ab_ref[...] = lax.dot_general(n, wab.astype(BF16), nt_dims,
                                  preferred_element_type=F32)


def _small_proj(n, wt, f_bias, *, seq, f_col, ab_col):
    lp, d = n.shape
    heads = f_bias.shape[0]
    assert heads == FOX_HEADS and 3 * heads <= LANE
    assert f_col % LANE + heads <= LANE and ab_col % LANE + GLA_GATE_RANK <= LANE
    tsz = META_BLOCK
    n_x = seq // tsz
    fb3 = jnp.concatenate([f_bias, f_bias, f_bias,
                           jnp.zeros((LANE - 3 * heads,), f_bias.dtype)])

    def row_block(t):
        return jnp.where(t == 0, n_x, t - 1)

    return pl.pallas_call(
        functools.partial(_small_proj_body, f_col % LANE, ab_col % LANE),
        grid=(n_x + 1,),
        in_specs=[pl.BlockSpec((tsz, d), lambda t: (row_block(t), 0)),
                  pl.BlockSpec((LANE, d), lambda t: (f_col // LANE, 0)),
                  pl.BlockSpec((1, LANE), lambda t: (0, 0)),
                  pl.BlockSpec((LANE, d), lambda t: (ab_col // LANE, 0))],
        out_specs=[pl.BlockSpec((tsz, LANE), lambda t: (row_block(t), 0)),
                   pl.BlockSpec((tsz, LANE), lambda t: (row_block(t), 0))],
        out_shape=[jax.ShapeDtypeStruct((lp, LANE), BF16),
                   jax.ShapeDtypeStruct((lp, LANE), F32)],
        scratch_shapes=[pltpu.VMEM((8, LANE), F32)],
        compiler_params=pltpu.CompilerParams(
            dimension_semantics=("arbitrary",), vmem_limit_bytes=VMEM_LIMIT),
        name="small_proj",
    )(n, wt, fb3.reshape(1, LANE), wt)


def _fox_body(tq, rq, hb, i_tab, j_tab, q_ref, km_ref, vm_ref, am_ref, kx_ref, vx_ref,
              ax_ref, o_ref, m_ref, acc_ref):
    g = pl.program_id(0)
    t = pl.program_id(1)
    i = i_tab[t]
    j = j_tab[t]
    dh = LANE
    lane = lax.broadcasted_iota(jnp.int32, (rq, LANE), 1)

    def q_ones(hh):
        h = g * hb + hh
        pick = jnp.logical_or(lane == h, jnp.logical_or(lane == h + FOX_HEADS,
                                                        lane == h + 2 * FOX_HEADS))
        return jnp.where(pick, 1.0, 0.0).astype(BF16)

    def online_update(k_ref, v_ref, a_ref, diagonal):
        n_k = k_ref.shape[0]
        ones = jnp.ones((n_k, LANE), BF16)
        aug = a_ref[...]
        cols_of = [slice(hh * dh, (hh + 1) * dh) for hh in range(hb)]
        keys_of = [jnp.concatenate([k_ref[:, c], aug], axis=1) for c in cols_of]
        vals_of = [jnp.concatenate([v_ref[:, c], ones], axis=1) for c in cols_of]
        pick_of = [q_ones(hh) for hh in range(hb)]
        for r in range(tq // rq):
            for hh in range(hb):
                cols, keys, vals, pick = cols_of[hh], keys_of[hh], vals_of[hh], pick_of[hh]
                rows = pl.ds(r * rq, rq)
                n_c = (r + 1) * rq if diagonal else n_k
                q = jnp.concatenate([q_ref[rows, cols], pick], axis=1)
                s = lax.dot_general(q, keys[:n_c], (((1,), (1,)), ((), ())),
                                    preferred_element_type=F32)
                if diagonal:
                    tri = (lax.broadcasted_iota(jnp.int32, (rq, rq), 1)
                           <= lax.broadcasted_iota(jnp.int32, (rq, rq), 0))
                    s_last = jnp.where(tri, s[:, n_c - rq:], NEG_BIG)
                    s = s_last if r == 0 else jnp.concatenate(
                        [s[:, :n_c - rq], s_last], axis=1)
                m_prev = m_ref[hh, rows, :]
                m_next = jnp.maximum(m_prev, jnp.max(s, axis=1, keepdims=True))
                p = jnp.exp2(s - jnp.tile(m_next, (1, n_c // LANE)))
                alpha = jnp.exp2(m_prev - m_next)
                m_ref[hh, rows, :] = m_next
                acc_ref[hh, rows, :] = (jnp.tile(alpha, (1, 2)) * acc_ref[hh, rows, :]
                                        + jnp.dot(p.astype(BF16), vals[:n_c],
                                                  preferred_element_type=F32))

    @pl.when(j == 0)
    def _():
        m_ref[...] = jnp.full_like(m_ref, NEG_BIG)
        acc_ref[...] = jnp.zeros_like(acc_ref)
        online_update(km_ref, vm_ref, am_ref, False)

    @pl.when(jnp.logical_and(j >= 1, j - 1 < i))
    def _():
        online_update(kx_ref, vx_ref, ax_ref, False)

    @pl.when(j - 1 == i)
    def _():
        online_update(kx_ref, vx_ref, ax_ref, True)
        for hh in range(hb):
            acc = acc_ref[hh]
            o_ref[:, hh * dh:(hh + 1) * dh] = (acc[:, :dh] / acc[:, dh:]).astype(
                o_ref.dtype)


def _fox_attention(qkv, kaug, *, seq, tq, rq, hb):
    heads = FOX_HEADS
    dh = FOX_HEAD_DIM
    assert dh == LANE and seq % tq == 0 and tq % rq == 0 and rq % LANE == 0
    assert heads % hb == 0
    n_q = seq // tq
    n_g = heads // hb
    meta_blk = seq // META_BLOCK
    steps = [(i, j) for i in range(n_q) for j in range(i + 2)]
    i_tab = jnp.asarray([s[0] for s in steps], jnp.int32)
    j_tab = jnp.asarray([s[1] for s in steps], jnp.int32)
    wb = hb * dh

    def kv_block(t, j_tab):
        return jnp.maximum(j_tab[t] - 1, 0)

    grid_spec = pltpu.PrefetchScalarGridSpec(
        num_scalar_prefetch=2,
        grid=(n_g, len(steps)),
        in_specs=[
            pl.BlockSpec((tq, wb), lambda g, t, it, jt: (it[t], g)),
            pl.BlockSpec((META_BLOCK, wb), lambda g, t, it, jt: (meta_blk, n_g + g)),
            pl.BlockSpec((META_BLOCK, wb),
                         lambda g, t, it, jt: (meta_blk, 2 * n_g + g)),
            pl.BlockSpec((META_BLOCK, LANE), lambda g, t, it, jt: (meta_blk, 0)),
            pl.BlockSpec((tq, wb), lambda g, t, it, jt: (kv_block(t, jt), n_g + g)),
            pl.BlockSpec((tq, wb),
                         lambda g, t, it, jt: (kv_block(t, jt), 2 * n_g + g)),
            pl.BlockSpec((tq, LANE), lambda g, t, it, jt: (kv_block(t, jt), 0)),
        ],
        out_specs=pl.BlockSpec((tq, wb), lambda g, t, it, jt: (it[t], g)),
        scratch_shapes=[pltpu.VMEM((hb, tq, LANE), F32),
                        pltpu.VMEM((hb, tq, 2 * dh), F32)],
    )
    return pl.pallas_call(
        functools.partial(_fox_body, tq, rq, hb),
        grid_spec=grid_spec,
        out_shape=jax.ShapeDtypeStruct((seq, heads * dh), BF16),
        compiler_params=pltpu.CompilerParams(
            dimension_semantics=("parallel", "arbitrary"),
            vmem_limit_bytes=VMEM_LIMIT),
        name="fox_attention",
    )(i_tab, j_tab, qkv, qkv, qkv, kaug, qkv, qkv, kaug)


def _gla_body(qv_ref, kr_ref, ab_ref, w2_ref, b_ref, g_ref, o_ref, state_ref):
    t = pl.program_id(0)
    is_meta = t == 0

    @pl.when(is_meta)
    def _():
        state_ref[...] = jnp.zeros_like(state_ref)

    z = jnp.dot(ab_ref[...].astype(BF16), w2_ref[...].astype(BF16),
                preferred_element_type=F32) + b_ref[...]
    la = _log_sigmoid(z) / GLA_GATE_TAU
    row = lax.broadcasted_iota(jnp.int32, la.shape, 0)
    la = jnp.where(jnp.logical_and(is_meta, row >= N_META), 0.0, la)
    c = la.shape[0]
    lower = (lax.broadcasted_iota(jnp.int32, (c, c), 1)
             <= lax.broadcasted_iota(jnp.int32, (c, c), 0)).astype(F32)
    cum = jnp.dot(lower, la, precision=lax.Precision.HIGHEST,
                  preferred_element_type=F32)
    total = cum[c - 1:c, :]
    k_dec = (kr_ref[:, :GLA_KW] * jnp.exp(total - cum)).astype(BF16)
    decay = jnp.exp(total)

    for h in range(GLA_HEADS):
        ks = slice(h * GLA_DK, (h + 1) * GLA_DK)
        vs = slice(GLA_KW + h * GLA_DV, GLA_KW + (h + 1) * GLA_DV)
        kv = lax.dot_general(qv_ref[:, vs], k_dec[:, ks], (((0,), (0,)), ((), ())),
                             preferred_element_type=F32)
        s_new = state_ref[h] * decay[:, ks] + kv
        state_ref[h] = s_new
        o = lax.dot_general(qv_ref[:, ks], s_new.astype(BF16),
                            (((1,), (1,)), ((), ())),
                            preferred_element_type=F32)
        ms = jnp.mean(o * o, axis=-1, keepdims=True)
        r = kr_ref[:, vs]
        o_ref[:, h * GLA_DV:(h + 1) * GLA_DV] = (
            o * lax.rsqrt(ms + EPS) * g_ref[...] * (r * _sigmoid(r))
        ).astype(o_ref.dtype)


def _gla(qv, kr, ab, w2, b, g, *, seq, ab_off):
    n_x = seq // CHUNK
    w2 = jnp.zeros((LANE, GLA_KW), w2.dtype).at[ab_off:ab_off + GLA_GATE_RANK].set(w2)

    def row_block(t):
        return jnp.where(t == 0, n_x, t - 1)

    width = GLA_KW + GLA_VW
    return pl.pallas_call(
        _gla_body,
        grid=(n_x + 1,),
        in_specs=[pl.BlockSpec((CHUNK, width), lambda t: (row_block(t), 0)),
                  pl.BlockSpec((CHUNK, width), lambda t: (row_block(t), 0)),
                  pl.BlockSpec((CHUNK, LANE), lambda t: (row_block(t), 0)),
                  pl.BlockSpec((LANE, GLA_KW), lambda t: (0, 0)),
                  pl.BlockSpec((1, GLA_KW), lambda t: (0, 0)),
                  pl.BlockSpec((1, GLA_DV), lambda t: (0, 0))],
        out_specs=pl.BlockSpec((CHUNK, GLA_VW), lambda t: (jnp.maximum(t - 1, 0), 0)),
        out_shape=jax.ShapeDtypeStruct((seq, GLA_VW), BF16),
        scratch_shapes=[pltpu.VMEM((GLA_HEADS, GLA_DV, GLA_DK), F32)],
        compiler_params=pltpu.CompilerParams(
            dimension_semantics=("arbitrary",), vmem_limit_bytes=VMEM_LIMIT),
        name="gla",
    )(qv, kr, ab, w2, b.reshape(1, GLA_KW), g.reshape(1, GLA_DV))


def kernel(x, meta_tokens, norm1_g, ffn1_w1, ffn1_w3, ffn1_w2, norm_mix_g, w_in,
           fox_f_bias, gla_alpha_w2, gla_alpha_b, gla_norm_g, w_gate, w_proj_fox,
           w_proj_gla, w_out, norm2_g, ffn2_w1, ffn2_w3, ffn2_w2, norm_final_g):
    batch, seq, d = x.shape
    assert batch == 1 and seq % 1024 == 0
    depth = norm1_g.shape[0]
    lp = seq + META_BLOCK
    assert lp % 8 == 0 and (lp // 8) % 16 == 0
    tm_all = lp // 8
    tm_x = seq // 8

    norm_tile = 2 * META_BLOCK
    meta_pad = jnp.concatenate(
        [meta_tokens.astype(x.dtype), jnp.zeros((norm_tile - N_META, d), x.dtype)], axis=0)
    x2 = x[0]

    o_fa = FOX_WIDTH * 3
    o_qb = o_fa + FOX_HEADS
    o_kb = o_qb + GLA_KW
    o_vb = o_kb + GLA_KW
    o_rb = o_vb + GLA_VW
    o_ab = o_rb + GLA_VW
    fox_scale = FOX_HEAD_DIM ** -0.5 * LOG2E
    gla_scale = GLA_DK ** -0.5

    assert depth == 1
    for li in range(depth):
        n1 = _rmsnorm_frames_meta(x2, meta_pad, norm1_g[li], lp=lp)
        h = _swiglu_ffn(n1, x2, ffn1_w1[li], ffn1_w3[li], ffn1_w2[li], m_rows=lp,
                        tm_up=2 * tm_all, tm_down=2 * tm_all,
                        res_tail=meta_pad[:META_BLOCK])

        n = _rmsnorm(h, norm_mix_g[li], lp, BF16)
        wt = jnp.transpose(w_in[li])

        tn_p = 512
        shift = o_qb % tn_p

        def tiles(start, width):
            assert start % tn_p in (0, shift) and width % tn_p == 0
            return [start - start % tn_p + r for r in range(0, width, tn_p)]

        def fox_epilogue(dot, j):
            return dot * jnp.where(j < FOX_WIDTH // tn_p, fox_scale, 1.0)

        qkv_a = _matmul_nt(n, wt, tiles(0, 3 * FOX_WIDTH), 0, fox_epilogue, m_rows=lp,
                           out_dtype=BF16, tm=tm_all, tn=tn_p, name="proj_fox")

        def gla_q_epilogue(dot, j):
            return dot * jnp.where(j < GLA_KW // tn_p, gla_scale, 1.0)

        qv_b = _matmul_nt(n, wt, tiles(o_qb, GLA_KW) + tiles(o_vb, GLA_VW), shift,
                          gla_q_epilogue, m_rows=lp, out_dtype=BF16,
                          tm=tm_all, tn=tn_p, name="proj_gla_qv")
        kr_b = _matmul_nt(n, wt, tiles(o_kb, GLA_KW) + tiles(o_rb, GLA_VW), shift,
                          lambda dot, j: dot, m_rows=lp, out_dtype=F32,
                          tm=tm_all, tn=tn_p, name="proj_gla_kr")

        kaug, ab = _small_proj(n, wt, fox_f_bias[li], seq=seq, f_col=o_fa, ab_col=o_ab)
        o_a = _fox_attention(qkv_a, kaug, seq=seq, tq=2048, rq=256, hb=2)
        o_b = _gla(qv_b, kr_b, ab, gla_alpha_w2[li], gla_alpha_b[li], gla_norm_g[li],
                   seq=seq, ab_off=o_ab % LANE)

        wg = w_gate[li]

        def mix_epilogue(accs, extras, j):
            ga, gb, pa, pb = accs
            return _sigmoid(ga) * pa + _sigmoid(gb) * pb

        tn_m = 256
        y = _matmul_fullk(
            [n, o_a, o_b],
            [(0, wg, 0), (0, wg, d // tn_m), (1, w_proj_fox[li], 0),
             (2, w_proj_gla[li], 0)],
            [], mix_epilogue, m_rows=seq, n_cols=d, out_dtype=BF16,
            tm=tm_x, tn=tn_m, name="mix", a_buffers=1)

        h = _matmul_fullk([y], [(0, w_out[li], 0)], [h],
                          lambda a, e, j: e[0] + a[0], m_rows=seq, n_cols=d,
                          out_dtype=F32, tm=tm_x, tn=512, name="out_proj")

        n2 = _rmsnorm(h, norm2_g[li], seq, BF16)
        h = _swiglu_ffn(n2, h, ffn2_w1[li], ffn2_w3[li], ffn2_w2[li], m_rows=seq,
                        tm_up=2 * tm_x, tm_down=2 * tm_x)

    out = _rmsnorm(h, norm_final_g, seq, F32)
    return out.reshape(batch, seq, d)
```

```python
import functools

import jax
import jax.numpy as jnp
from jax import lax
from jax.experimental import pallas as pl
from jax.experimental.pallas import tpu as pltpu

F32 = jnp.float32
BF16 = jnp.bfloat16

EPS = 1e-6
N_META = 16
CHUNK = 64
FOX_HEADS = 16
FOX_HEAD_DIM = 128
FOX_WIDTH = FOX_HEADS * FOX_HEAD_DIM
GLA_HEADS = 4
GLA_DK = 256
GLA_DV = 512
GLA_KW = GLA_HEADS * GLA_DK
GLA_VW = GLA_HEADS * GLA_DV
GLA_GATE_RANK = 16
GLA_GATE_TAU = 16.0

LANE = 128
META_BLOCK = LANE
NEG_BIG = -1e30
LOG2E = 1.4426950408889634
VMEM_LIMIT = 60 * 1024 * 1024


def _log_sigmoid(z):
    return jnp.minimum(z, 0.0) - jnp.log(1.0 + jnp.exp(-jnp.abs(z)))


def _sigmoid(z):
    return 1.0 / (1.0 + jnp.exp(-z))


def _tri_matmul(tri, x):
    hi = x.astype(BF16)
    rest = x - hi.astype(F32)
    mid = rest.astype(BF16)
    lo = (rest - mid.astype(F32)).astype(BF16)
    return (jnp.dot(tri, hi, preferred_element_type=F32)
            + jnp.dot(tri, mid, preferred_element_type=F32)
            + jnp.dot(tri, lo, preferred_element_type=F32))


def _rmsnorm_body(x_ref, g_ref, o_ref):
    x = x_ref[...]
    ms = jnp.mean(x * x, axis=-1, keepdims=True)
    o_ref[...] = (x * lax.rsqrt(ms + EPS) * g_ref[...]).astype(o_ref.dtype)


def _rmsnorm(x, g, rows, out_dtype):
    d = x.shape[1]
    tile = max(t for t in range(16, 513, 16) if rows % t == 0)
    return pl.pallas_call(
        _rmsnorm_body,
        grid=(rows // tile,),
        in_specs=[pl.BlockSpec((tile, d), lambda i: (i, 0)),
                  pl.BlockSpec((1, d), lambda i: (0, 0))],
        out_specs=pl.BlockSpec((tile, d), lambda i: (i, 0)),
        out_shape=jax.ShapeDtypeStruct((rows, d), out_dtype),
        compiler_params=pltpu.CompilerParams(
            dimension_semantics=("parallel",), vmem_limit_bytes=VMEM_LIMIT),
        name="rmsnorm",
    )(x, g.reshape(1, d))


def _rmsnorm_frames_meta_body(n_x, x_ref, mp_ref, g_ref, o_ref):
    src = jnp.where(pl.program_id(0) < n_x, x_ref[...], mp_ref[...])
    ms = jnp.mean(src * src, axis=-1, keepdims=True)
    o_ref[...] = (src * lax.rsqrt(ms + EPS) * g_ref[...]).astype(o_ref.dtype)


def _rmsnorm_frames_meta(x, meta_pad, g, *, lp):
    seq, d = x.shape
    tile = meta_pad.shape[0]
    assert seq % tile == 0 and seq < lp <= seq + tile
    n_x = seq // tile
    return pl.pallas_call(
        functools.partial(_rmsnorm_frames_meta_body, n_x),
        grid=(n_x + 1,),
        in_specs=[pl.BlockSpec((tile, d), lambda i: (jnp.minimum(i, n_x - 1), 0)),
                  pl.BlockSpec((tile, d), lambda i: (0, 0)),
                  pl.BlockSpec((1, d), lambda i: (0, 0))],
        out_specs=pl.BlockSpec((tile, d), lambda i: (i, 0)),
        out_shape=jax.ShapeDtypeStruct((lp, d), BF16),
        compiler_params=pltpu.CompilerParams(
            dimension_semantics=("parallel",), vmem_limit_bytes=VMEM_LIMIT),
        name="rmsnorm_in",
    )(x, meta_pad, g.reshape(1, d))


def _mm_fullk_body(pair_ai, n_a, n_extra, epilogue, *refs):
    n_p = len(pair_ai)
    a_refs = refs[:n_a]
    b_refs = refs[n_a:n_a + n_p]
    e_refs = refs[n_a + n_p:n_a + n_p + n_extra]
    o_ref = refs[n_a + n_p + n_extra]
    dots = [jnp.dot(a_refs[ai][...], b_refs[p][...].astype(BF16),
                    preferred_element_type=F32) for p, ai in enumerate(pair_ai)]
    extras = [r[...] for r in e_refs]
    o_ref[...] = epilogue(dots, extras, pl.program_id(1)).astype(o_ref.dtype)


def _matmul_fullk(a_list, pairs, extras, epilogue, *, m_rows, n_cols, out_dtype,
                  tm, tn, name, a_buffers=2):
    assert m_rows % tm == 0 and n_cols % tn == 0
    a_mode = {} if a_buffers == 2 else {"pipeline_mode": pl.Buffered(a_buffers)}
    in_specs = [pl.BlockSpec((tm, a.shape[1]), lambda i, j: (i, 0), **a_mode)
                for a in a_list]
    for ai, b, col_off in pairs:
        assert b.shape[0] == a_list[ai].shape[1]
        in_specs.append(pl.BlockSpec((b.shape[0], tn),
                                     lambda i, j, col_off=col_off: (0, j + col_off)))
    for _ in extras:
        in_specs.append(pl.BlockSpec((tm, tn), lambda i, j: (i, j)))
    body = functools.partial(_mm_fullk_body, tuple(ai for ai, _, _ in pairs),
                             len(a_list), len(extras), epilogue)
    return pl.pallas_call(
        body,
        grid=(m_rows // tm, n_cols // tn),
        in_specs=in_specs,
        out_specs=pl.BlockSpec((tm, tn), lambda i, j: (i, j)),
        out_shape=jax.ShapeDtypeStruct((m_rows, n_cols), out_dtype),
        compiler_params=pltpu.CompilerParams(
            dimension_semantics=("parallel", "arbitrary"),
            vmem_limit_bytes=VMEM_LIMIT),
        name=name,
    )(*a_list, *[b for _, b, _ in pairs], *extras)


def _mm_nt_body(shift, epilogue, tab_ref, a_ref, bm_ref, *rest):
    del tab_ref
    b = bm_ref[...]
    if shift:
        bn_ref, o_ref = rest
        b = jnp.concatenate([b[shift:], bn_ref[...]], axis=0)
    else:
        (o_ref,) = rest
    d = lax.dot_general(a_ref[...], b.astype(BF16), (((1,), (1,)), ((), ())),
                        preferred_element_type=F32)
    o_ref[...] = epilogue(d, pl.program_id(1)).astype(o_ref.dtype)


def _matmul_nt(a, bt, tile_starts, shift, epilogue, *, m_rows, out_dtype, tm, tn, name):
    assert m_rows % tm == 0 and all(s % tn == 0 for s in tile_starts)
    assert shift % 8 == 0 and (shift == 0 or tn % shift == 0)
    k_total = a.shape[1]
    assert bt.shape[1] == k_total
    tab = jnp.asarray([s // tn for s in tile_starts], jnp.int32)
    in_specs = [pl.BlockSpec((tm, k_total), lambda i, j, tab: (i, 0)),
                pl.BlockSpec((tn, k_total), lambda i, j, tab: (tab[j], 0))]
    operands = [tab, a, bt]
    if shift:
        per = tn // shift
        in_specs.append(pl.BlockSpec((shift, k_total),
                                     lambda i, j, tab: ((tab[j] + 1) * per, 0)))
        operands.append(bt)
    grid_spec = pltpu.PrefetchScalarGridSpec(
        num_scalar_prefetch=1,
        grid=(m_rows // tm, len(tile_starts)),
        in_specs=in_specs,
        out_specs=pl.BlockSpec((tm, tn), lambda i, j, tab: (i, j)),
    )
    return pl.pallas_call(
        functools.partial(_mm_nt_body, shift, epilogue),
        grid_spec=grid_spec,
        out_shape=jax.ShapeDtypeStruct((m_rows, tn * len(tile_starts)), out_dtype),
        compiler_params=pltpu.CompilerParams(
            dimension_semantics=("parallel", "arbitrary"),
            vmem_limit_bytes=VMEM_LIMIT),
        name=name,
    )(*operands)


def _mm_kacc_body(tk, k_total, scale, has_tail, a_ref, b_ref, r_ref, *rest):
    if has_tail:
        tail_ref, o_ref = rest
    else:
        (o_ref,) = rest
    k = pl.program_id(2)
    last_row_block = pl.program_id(0) == pl.num_programs(0) - 1

    def residual():
        r = r_ref[...]
        if has_tail:
            tm, t_rows = r.shape[0], tail_ref.shape[0]
            tail = jnp.concatenate(
                [jnp.zeros((tm - t_rows, r.shape[1]), r.dtype), tail_ref[...]], axis=0)
            row = lax.broadcasted_iota(jnp.int32, r.shape, 0)
            in_tail = jnp.logical_and(last_row_block, row >= tm - t_rows)
            r = jnp.where(in_tail, tail, r)
        return r

    def partial_product():
        a = a_ref[...]
        b = b_ref[...]
        if k_total % tk:
            limit = k_total - k * tk
            a = jnp.where(lax.broadcasted_iota(jnp.int32, a.shape, 1) < limit, a,
                          jnp.zeros_like(a))
            b = jnp.where(lax.broadcasted_iota(jnp.int32, b.shape, 0) < limit, b,
                          jnp.zeros_like(b))
        return scale * jnp.dot(a, b.astype(BF16), preferred_element_type=F32)

    @pl.when(k == 0)
    def _():
        o_ref[...] = residual() + partial_product()

    @pl.when(k > 0)
    def _():
        o_ref[...] += partial_product()


def _matmul_kacc(a, b, res, scale, *, m_rows, tm, tn, tk, name, res_tail=None):
    k_total, n_cols = b.shape
    assert a.shape[1] == k_total and m_rows % tm == 0 and n_cols % tn == 0
    in_specs = [pl.BlockSpec((tm, tk), lambda i, j, k: (i, k)),
                pl.BlockSpec((tk, tn), lambda i, j, k: (k, j)),
                pl.BlockSpec((tm, tn), lambda i, j, k: (i, j))]
    operands = [a, b, res]
    if res_tail is not None:
        assert res.shape[0] + res_tail.shape[0] == m_rows
        in_specs.append(pl.BlockSpec((res_tail.shape[0], tn), lambda i, j, k: (0, j)))
        operands.append(res_tail)
    return pl.pallas_call(
        functools.partial(_mm_kacc_body, tk, k_total, scale, res_tail is not None),
        grid=(m_rows // tm, n_cols // tn, pl.cdiv(k_total, tk)),
        in_specs=in_specs,
        out_specs=pl.BlockSpec((tm, tn), lambda i, j, k: (i, j)),
        out_shape=jax.ShapeDtypeStruct((m_rows, n_cols), F32),
        compiler_params=pltpu.CompilerParams(
            dimension_semantics=("parallel", "parallel", "arbitrary"),
            vmem_limit_bytes=VMEM_LIMIT),
        name=name,
    )(*operands)


def _swiglu_ffn(n, h_res, w1, w3, w2, *, m_rows, tm_up, tm_down, res_tail=None):
    def up_epilogue(dots, extras, j):
        a1, a3 = dots
        return a1 * _sigmoid(a1) * a3

    u = _matmul_fullk([n], [(0, w1, 0), (0, w3, 0)], [], up_epilogue,
                      m_rows=m_rows, n_cols=w1.shape[1], out_dtype=BF16,
                      tm=tm_up, tn=256, name="ffn_up", a_buffers=1)
    return _matmul_kacc(u, w2, h_res, 0.5, m_rows=m_rows, tm=tm_down, tn=1024, tk=1024,
                        name="ffn_down", res_tail=res_tail)


def _small_proj_body(f_off, ab_off, n_ref, wf_ref, fb_ref, wab_ref, kaug_ref, ab_ref,
                     carry_ref):
    t = pl.program_id(0)
    is_meta = t == 0
    nt_dims = (((1,), (1,)), ((), ()))

    @pl.when(is_meta)
    def _():
        carry_ref[...] = jnp.zeros_like(carry_ref)

    n = n_ref[...]
    ft = lax.dot_general(n, wf_ref[...].astype(BF16), nt_dims,
                         preferred_element_type=F32)
    row = lax.broadcasted_iota(jnp.int32, ft.shape, 0)
    col = lax.broadcasted_iota(jnp.int32, ft.shape, 1)
    ft = jnp.where(jnp.logical_and(col >= f_off, col < f_off + FOX_HEADS), ft, 0.0)
    shifts = [(p * FOX_HEADS - f_off) % LANE for p in range(3)]
    ft = sum(pltpu.roll(ft, s, 1) if s else ft for s in shifts)
    logf = _log_sigmoid(ft + fb_ref[...])
    valid = jnp.logical_or(jnp.logical_not(is_meta), row < N_META)
    logf = jnp.where(valid, logf, 0.0)
    tsz = logf.shape[0]
    lower = jnp.where(lax.broadcasted_iota(jnp.int32, (tsz, tsz), 1)
                      <= lax.broadcasted_iota(jnp.int32, (tsz, tsz), 0),
                      1.0, 0.0).astype(BF16)
    c = carry_ref[0:1, :] + _tri_matmul(lower, logf)
    carry_ref[...] = carry_ref[...] + jnp.sum(logf, axis=0, keepdims=True)

    bias = -LOG2E * c
    hi = bias.astype(BF16).astype(F32)
    mid = (bias - hi).astype(BF16).astype(F32)
    lo = (bias - hi - mid).astype(BF16).astype(F32)
    pieces = jnp.where(col < FOX_HEADS, hi,
                       jnp.where(col < 2 * FOX_HEADS, mid,
                                 jnp.where(col < 3 * FOX_HEADS, lo, 0.0)))
    removed = jnp.where(col < FOX_HEADS, NEG_BIG, 0.0)
    kaug_ref[...] = jnp.where(valid, pieces, removed).astype(BF16)
    wrow = lax.broadcasted_iota(jnp.int32, wab_ref.shape, 0)
    wab = jnp.where(jnp.logical_and(wrow >= ab_off, wrow < ab_off + GLA_GATE_RANK),
                    wab_ref[...], 0.0)
    ab_ref[...] = lax.dot_general(n, wab.astype(BF16), nt_dims,
                                  preferred_element_type=F32)


def _small_proj(n, wt, f_bias, *, seq, f_col, ab_col):
    lp, d = n.shape
    heads = f_bias.shape[0]
    assert heads == FOX_HEADS and 3 * heads <= LANE
    assert f_col % LANE + heads <= LANE and ab_col % LANE + GLA_GATE_RANK <= LANE
    tsz = 4 * META_BLOCK
    assert seq % tsz == 0
    n_x = seq // tsz
    fb3 = jnp.concatenate([f_bias, f_bias, f_bias,
                           jnp.zeros((LANE - 3 * heads,), f_bias.dtype)])

    def row_block(t):
        return jnp.where(t == 0, n_x, t - 1)

    return pl.pallas_call(
        functools.partial(_small_proj_body, f_col % LANE, ab_col % LANE),
        grid=(n_x + 1,),
        in_specs=[pl.BlockSpec((tsz, d), lambda t: (row_block(t), 0)),
                  pl.BlockSpec((LANE, d), lambda t: (f_col // LANE, 0)),
                  pl.BlockSpec((1, LANE), lambda t: (0, 0)),
                  pl.BlockSpec((LANE, d), lambda t: (ab_col // LANE, 0))],
        out_specs=[pl.BlockSpec((tsz, LANE), lambda t: (row_block(t), 0)),
                   pl.BlockSpec((tsz, LANE), lambda t: (row_block(t), 0))],
        out_shape=[jax.ShapeDtypeStruct((lp, LANE), BF16),
                   jax.ShapeDtypeStruct((lp, LANE), F32)],
        scratch_shapes=[pltpu.VMEM((8, LANE), F32)],
        compiler_params=pltpu.CompilerParams(
            dimension_semantics=("arbitrary",), vmem_limit_bytes=VMEM_LIMIT),
        name="small_proj",
    )(n, wt, fb3.reshape(1, LANE), wt)


def _fox_body(tq, rq, hb, i_tab, j_tab, q_ref, km_ref, vm_ref, am_ref, kx_ref, vx_ref,
              ax_ref, o_ref, m_ref, acc_ref):
    g = pl.program_id(0)
    t = pl.program_id(1)
    i = i_tab[t]
    j = j_tab[t]
    dh = LANE
    lane = lax.broadcasted_iota(jnp.int32, (rq, LANE), 1)

    def q_ones(hh):
        h = g * hb + hh
        pick = jnp.logical_or(lane == h, jnp.logical_or(lane == h + FOX_HEADS,
                                                        lane == h + 2 * FOX_HEADS))
        return jnp.where(pick, 1.0, 0.0).astype(BF16)

    def online_update(k_ref, v_ref, a_ref, diagonal):
        n_k = k_ref.shape[0]
        ones = jnp.ones((n_k, LANE), BF16)
        aug = a_ref[...]
        cols_of = [slice(hh * dh, (hh + 1) * dh) for hh in range(hb)]
        keys_of = [jnp.concatenate([k_ref[:, c], aug], axis=1) for c in cols_of]
        vals_of = [jnp.concatenate([v_ref[:, c], ones], axis=1) for c in cols_of]
        pick_of = [q_ones(hh) for hh in range(hb)]
        for r in range(tq // rq):
            for hh in range(hb):
                cols, keys, vals, pick = cols_of[hh], keys_of[hh], vals_of[hh], pick_of[hh]
                rows = pl.ds(r * rq, rq)
                n_c = (r + 1) * rq if diagonal else n_k
                q = jnp.concatenate([q_ref[rows, cols], pick], axis=1)
                s = lax.dot_general(q, keys[:n_c], (((1,), (1,)), ((), ())),
                                    preferred_element_type=F32)
                if diagonal:
                    tri = (lax.broadcasted_iota(jnp.int32, (rq, rq), 1)
                           <= lax.broadcasted_iota(jnp.int32, (rq, rq), 0))
                    s_last = jnp.where(tri, s[:, n_c - rq:], NEG_BIG)
                    s = s_last if r == 0 else jnp.concatenate(
                        [s[:, :n_c - rq], s_last], axis=1)
                m_prev = m_ref[hh, rows, :]
                m_next = jnp.maximum(m_prev, jnp.max(s, axis=1, keepdims=True))
                p = jnp.exp2(s - jnp.tile(m_next, (1, n_c // LANE)))
                alpha = jnp.exp2(m_prev - m_next)
                m_ref[hh, rows, :] = m_next
                acc_ref[hh, rows, :] = (jnp.tile(alpha, (1, 2)) * acc_ref[hh, rows, :]
                                        + jnp.dot(p.astype(BF16), vals[:n_c],
                                                  preferred_element_type=F32))

    @pl.when(j == 0)
    def _():
        m_ref[...] = jnp.full_like(m_ref, NEG_BIG)
        acc_ref[...] = jnp.zeros_like(acc_ref)
        online_update(km_ref, vm_ref, am_ref, False)

    @pl.when(jnp.logical_and(j >= 1, j - 1 < i))
    def _():
        online_update(kx_ref, vx_ref, ax_ref, False)

    @pl.when(j - 1 == i)
    def _():
        online_update(kx_ref, vx_ref, ax_ref, True)
        for hh in range(hb):
            acc = acc_ref[hh]
            o_ref[:, hh * dh:(hh + 1) * dh] = (acc[:, :dh] / acc[:, dh:]).astype(
                o_ref.dtype)


def _fox_attention(qkv, kaug, *, seq, tq, rq, hb):
    heads = FOX_HEADS
    dh = FOX_HEAD_DIM
    assert dh == LANE and seq % tq == 0 and tq % rq == 0 and rq % LANE == 0
    assert heads % hb == 0
    n_q = seq // tq
    n_g = heads // hb
    meta_blk = seq // META_BLOCK
    steps = [(i, j) for i in range(n_q) for j in range(i + 2)]
    i_tab = jnp.asarray([s[0] for s in steps], jnp.int32)
    j_tab = jnp.asarray([s[1] for s in steps], jnp.int32)
    wb = hb * dh

    def kv_block(t, j_tab):
        return jnp.maximum(j_tab[t] - 1, 0)

    grid_spec = pltpu.PrefetchScalarGridSpec(
        num_scalar_prefetch=2,
        grid=(n_g, len(steps)),
        in_specs=[
            pl.BlockSpec((tq, wb), lambda g, t, it, jt: (it[t], g)),
            pl.BlockSpec((META_BLOCK, wb), lambda g, t, it, jt: (meta_blk, n_g + g)),
            pl.BlockSpec((META_BLOCK, wb),
                         lambda g, t, it, jt: (meta_blk, 2 * n_g + g)),
            pl.BlockSpec((META_BLOCK, LANE), lambda g, t, it, jt: (meta_blk, 0)),
            pl.BlockSpec((tq, wb), lambda g, t, it, jt: (kv_block(t, jt), n_g + g)),
            pl.BlockSpec((tq, wb),
                         lambda g, t, it, jt: (kv_block(t, jt), 2 * n_g + g)),
            pl.BlockSpec((tq, LANE), lambda g, t, it, jt: (kv_block(t, jt), 0)),
        ],
        out_specs=pl.BlockSpec((tq, wb), lambda g, t, it, jt: (it[t], g)),
        scratch_shapes=[pltpu.VMEM((hb, tq, LANE), F32),
                        pltpu.VMEM((hb, tq, 2 * dh), F32)],
    )
    return pl.pallas_call(
        functools.partial(_fox_body, tq, rq, hb),
        grid_spec=grid_spec,
        out_shape=jax.ShapeDtypeStruct((seq, heads * dh), BF16),
        compiler_params=pltpu.CompilerParams(
            dimension_semantics=("parallel", "arbitrary"),
            vmem_limit_bytes=VMEM_LIMIT),
        name="fox_attention",
    )(i_tab, j_tab, qkv, qkv, qkv, kaug, qkv, qkv, kaug)


def _gla_body(n_chunks, qv_ref, kr_ref, ab_ref, w2_ref, b_ref, g_ref, o_ref, state_ref):
    t = pl.program_id(0)
    is_meta = t == 0

    @pl.when(is_meta)
    def _():
        state_ref[...] = jnp.zeros_like(state_ref)

    z = jnp.dot(ab_ref[...].astype(BF16), w2_ref[...].astype(BF16),
                preferred_element_type=F32) + b_ref[...]
    la = _log_sigmoid(z) / GLA_GATE_TAU
    n_rows = la.shape[0]
    row = lax.broadcasted_iota(jnp.int32, la.shape, 0)
    la = jnp.where(jnp.logical_and(is_meta, row >= N_META), 0.0, la)
    ri = lax.broadcasted_iota(jnp.int32, (n_rows, n_rows), 0)
    ci = lax.broadcasted_iota(jnp.int32, (n_rows, n_rows), 1)
    same_chunk = (ri // CHUNK) == (ci // CHUNK)
    lower = jnp.where(jnp.logical_and(same_chunk, ci <= ri), 1.0, 0.0).astype(BF16)
    cum = _tri_matmul(lower, la)
    totals = [cum[(c + 1) * CHUNK - 1:(c + 1) * CHUNK, :] for c in range(n_chunks)]
    total_rows = jnp.concatenate(
        [jnp.broadcast_to(tot, (CHUNK, tot.shape[1])) for tot in totals], axis=0)
    k_dec = (kr_ref[:, :GLA_KW] * jnp.exp(total_rows - cum)).astype(BF16)

    for c in range(n_chunks):
        rows = slice(c * CHUNK, (c + 1) * CHUNK)
        decay = jnp.exp(totals[c])
        for h in range(GLA_HEADS):
            ks = slice(h * GLA_DK, (h + 1) * GLA_DK)
            vs = slice(GLA_KW + h * GLA_DV, GLA_KW + (h + 1) * GLA_DV)
            kv = lax.dot_general(qv_ref[rows, vs], k_dec[rows, ks],
                                 (((0,), (0,)), ((), ())),
                                 preferred_element_type=F32)
            s_new = state_ref[h] * decay[:, ks] + kv
            state_ref[h] = s_new
            o = lax.dot_general(qv_ref[rows, ks], s_new.astype(BF16),
                                (((1,), (1,)), ((), ())),
                                preferred_element_type=F32)
            ms = jnp.mean(o * o, axis=-1, keepdims=True)
            r = kr_ref[rows, vs]
            o_ref[rows, h * GLA_DV:(h + 1) * GLA_DV] = (
                o * lax.rsqrt(ms + EPS) * g_ref[...] * (r * _sigmoid(r))
            ).astype(o_ref.dtype)


def _gla(qv, kr, ab, w2, b, g, *, seq, ab_off):
    rows = META_BLOCK
    assert rows % CHUNK == 0 and seq % rows == 0
    n_x = seq // rows
    w2 = jnp.zeros((LANE, GLA_KW), w2.dtype).at[ab_off:ab_off + GLA_GATE_RANK].set(w2)

    def row_block(t):
        return jnp.where(t == 0, n_x, t - 1)

    width = GLA_KW + GLA_VW
    return pl.pallas_call(
        functools.partial(_gla_body, rows // CHUNK),
        grid=(n_x + 1,),
        in_specs=[pl.BlockSpec((rows, width), lambda t: (row_block(t), 0)),
                  pl.BlockSpec((rows, width), lambda t: (row_block(t), 0)),
                  pl.BlockSpec((rows, LANE), lambda t: (row_block(t), 0)),
                  pl.BlockSpec((LANE, GLA_KW), lambda t: (0, 0)),
                  pl.BlockSpec((1, GLA_KW), lambda t: (0, 0)),
                  pl.BlockSpec((1, GLA_DV), lambda t: (0, 0))],
        out_specs=pl.BlockSpec((rows, GLA_VW), lambda t: (jnp.maximum(t - 1, 0), 0)),
        out_shape=jax.ShapeDtypeStruct((seq, GLA_VW), BF16),
        scratch_shapes=[pltpu.VMEM((GLA_HEADS, GLA_DV, GLA_DK), F32)],
        compiler_params=pltpu.CompilerParams(
            dimension_semantics=("arbitrary",), vmem_limit_bytes=VMEM_LIMIT),
        name="gla",
    )(qv, kr, ab, w2, b.reshape(1, GLA_KW), g.reshape(1, GLA_DV))


def kernel(x, meta_tokens, norm1_g, ffn1_w1, ffn1_w3, ffn1_w2, norm_mix_g, w_in,
           fox_f_bias, gla_alpha_w2, gla_alpha_b, gla_norm_g, w_gate, w_proj_fox,
           w_proj_gla, w_out, norm2_g, ffn2_w1, ffn2_w3, ffn2_w2, norm_final_g):
    batch, seq, d = x.shape
    assert batch == 1 and seq % 1024 == 0
    depth = norm1_g.shape[0]
    lp = seq + META_BLOCK
    assert lp % 8 == 0 and (lp // 8) % 16 == 0
    tm_all = lp // 8
    tm_x = seq // 8

    norm_tile = 2 * META_BLOCK
    meta_pad = jnp.concatenate(
        [meta_tokens.astype(x.dtype), jnp.zeros((norm_tile - N_META, d), x.dtype)], axis=0)
    x2 = x[0]

    o_fa = FOX_WIDTH * 3
    o_qb = o_fa + FOX_HEADS
    o_kb = o_qb + GLA_KW
    o_vb = o_kb + GLA_KW
    o_rb = o_vb + GLA_VW
    o_ab = o_rb + GLA_VW
    fox_scale = FOX_HEAD_DIM ** -0.5 * LOG2E
    gla_scale = GLA_DK ** -0.5

    assert depth == 1
    for li in range(depth):
        n1 = _rmsnorm_frames_meta(x2, meta_pad, norm1_g[li], lp=lp)
        h = _swiglu_ffn(n1, x2, ffn1_w1[li], ffn1_w3[li], ffn1_w2[li], m_rows=lp,
                        tm_up=2 * tm_all, tm_down=2 * tm_all,
                        res_tail=meta_pad[:META_BLOCK])

        n = _rmsnorm(h, norm_mix_g[li], lp, BF16)
        wt = jnp.transpose(w_in[li])

        tn_p = 512
        shift = o_qb % tn_p

        def tiles(start, width):
            assert start % tn_p in (0, shift) and width % tn_p == 0
            return [start - start % tn_p + r for r in range(0, width, tn_p)]

        def fox_epilogue(dot, j):
            return dot * jnp.where(j < FOX_WIDTH // tn_p, fox_scale, 1.0)

        qkv_a = _matmul_nt(n, wt, tiles(0, 3 * FOX_WIDTH), 0, fox_epilogue, m_rows=lp,
                           out_dtype=BF16, tm=tm_all, tn=tn_p, name="proj_fox")

        def gla_q_epilogue(dot, j):
            return dot * jnp.where(j < GLA_KW // tn_p, gla_scale, 1.0)

        qv_b = _matmul_nt(n, wt, tiles(o_qb, GLA_KW) + tiles(o_vb, GLA_VW), shift,
                          gla_q_epilogue, m_rows=lp, out_dtype=BF16,
                          tm=tm_all, tn=tn_p, name="proj_gla_qv")
        kr_b = _matmul_nt(n, wt, tiles(o_kb, GLA_KW) + tiles(o_rb, GLA_VW), shift,
                          lambda dot, j: dot, m_rows=lp, out_dtype=F32,
                          tm=tm_all, tn=tn_p, name="proj_gla_kr")

        kaug, ab = _small_proj(n, wt, fox_f_bias[li], seq=seq, f_col=o_fa, ab_col=o_ab)
        o_a = _fox_attention(qkv_a, kaug, seq=seq, tq=2048, rq=256, hb=2)
        o_b = _gla(qv_b, kr_b, ab, gla_alpha_w2[li], gla_alpha_b[li], gla_norm_g[li],
                   seq=seq, ab_off=o_ab % LANE)

        wg = w_gate[li]

        def mix_epilogue(accs, extras, j):
            ga, gb, pa, pb = accs
            return _sigmoid(ga) * pa + _sigmoid(gb) * pb

        tn_m = 256
        y = _matmul_fullk(
            [n, o_a, o_b],
            [(0, wg, 0), (0, wg, d // tn_m), (1, w_proj_fox[li], 0),
             (2, w_proj_gla[li], 0)],
            [], mix_epilogue, m_rows=seq, n_cols=d, out_dtype=BF16,
            tm=tm_x, tn=tn_m, name="mix", a_buffers=1)

        h = _matmul_fullk([y], [(0, w_out[li], 0)], [h],
                          lambda a, e, j: e[0] + a[0], m_rows=seq, n_cols=d,
                          out_dtype=F32, tm=tm_x, tn=512, name="out_proj")

        n2 = _rmsnorm(h, norm2_g[li], seq, BF16)
        h = _swiglu_ffn(n2, h, ffn2_w1[li], ffn2_w3[li], ffn2_w2[li], m_rows=seq,
                        tm_up=2 * tm_x, tm_down=2 * tm_x)

    out = _rmsnorm(h, norm_final_g, seq, F32)
    return out.reshape(batch, seq, d)
```

```python
import functools

import jax
import jax.numpy as jnp
from jax import lax
from jax.experimental import pallas as pl
from jax.experimental.pallas import tpu as pltpu

F32 = jnp.float32
BF16 = jnp.bfloat16

EPS = 1e-6
N_META = 16
CHUNK = 64
FOX_HEADS = 16
FOX_HEAD_DIM = 128
FOX_WIDTH = FOX_HEADS * FOX_HEAD_DIM
GLA_HEADS = 4
GLA_DK = 256
GLA_DV = 512
GLA_KW = GLA_HEADS * GLA_DK
GLA_VW = GLA_HEADS * GLA_DV
GLA_GATE_RANK = 16
GLA_GATE_TAU = 16.0

LANE = 128
META_BLOCK = LANE
NEG_BIG = -1e30
LOG2E = 1.4426950408889634
VMEM_LIMIT = 60 * 1024 * 1024


def _log_sigmoid(z):
    return jnp.minimum(z, 0.0) - jnp.log(1.0 + jnp.exp(-jnp.abs(z)))


def _sigmoid(z):
    return 1.0 / (1.0 + jnp.exp(-z))


def _tri_matmul(tri, x):
    hi = x.astype(BF16)
    rest = x - hi.astype(F32)
    mid = rest.astype(BF16)
    lo = (rest - mid.astype(F32)).astype(BF16)
    return (jnp.dot(tri, hi, preferred_element_type=F32)
            + jnp.dot(tri, mid, preferred_element_type=F32)
            + jnp.dot(tri, lo, preferred_element_type=F32))


def _rmsnorm_body(x_ref, g_ref, o_ref):
    x = x_ref[...]
    ms = jnp.mean(x * x, axis=-1, keepdims=True)
    o_ref[...] = (x * lax.rsqrt(ms + EPS) * g_ref[...]).astype(o_ref.dtype)


def _rmsnorm(x, g, rows, out_dtype):
    d = x.shape[1]
    tile = max(t for t in range(16, 513, 16) if rows % t == 0)
    return pl.pallas_call(
        _rmsnorm_body,
        grid=(rows // tile,),
        in_specs=[pl.BlockSpec((tile, d), lambda i: (i, 0)),
                  pl.BlockSpec((1, d), lambda i: (0, 0))],
        out_specs=pl.BlockSpec((tile, d), lambda i: (i, 0)),
        out_shape=jax.ShapeDtypeStruct((rows, d), out_dtype),
        compiler_params=pltpu.CompilerParams(
            dimension_semantics=("parallel",), vmem_limit_bytes=VMEM_LIMIT),
        name="rmsnorm",
    )(x, g.reshape(1, d))


def _rmsnorm_frames_meta_body(n_x, x_ref, mp_ref, g_ref, o_ref):
    src = jnp.where(pl.program_id(0) < n_x, x_ref[...], mp_ref[...])
    ms = jnp.mean(src * src, axis=-1, keepdims=True)
    o_ref[...] = (src * lax.rsqrt(ms + EPS) * g_ref[...]).astype(o_ref.dtype)


def _rmsnorm_frames_meta(x, meta_pad, g, *, lp):
    seq, d = x.shape
    tile = meta_pad.shape[0]
    assert seq % tile == 0 and seq < lp <= seq + tile
    n_x = seq // tile
    return pl.pallas_call(
        functools.partial(_rmsnorm_frames_meta_body, n_x),
        grid=(n_x + 1,),
        in_specs=[pl.BlockSpec((tile, d), lambda i: (jnp.minimum(i, n_x - 1), 0)),
                  pl.BlockSpec((tile, d), lambda i: (0, 0)),
                  pl.BlockSpec((1, d), lambda i: (0, 0))],
        out_specs=pl.BlockSpec((tile, d), lambda i: (i, 0)),
        out_shape=jax.ShapeDtypeStruct((lp, d), BF16),
        compiler_params=pltpu.CompilerParams(
            dimension_semantics=("parallel",), vmem_limit_bytes=VMEM_LIMIT),
        name="rmsnorm_in",
    )(x, meta_pad, g.reshape(1, d))


def _mm_fullk_body(pair_ai, n_a, n_extra, epilogue, *refs):
    n_p = len(pair_ai)
    a_refs = refs[:n_a]
    b_refs = refs[n_a:n_a + n_p]
    e_refs = refs[n_a + n_p:n_a + n_p + n_extra]
    o_ref = refs[n_a + n_p + n_extra]
    dots = [jnp.dot(a_refs[ai][...], b_refs[p][...].astype(BF16),
                    preferred_element_type=F32) for p, ai in enumerate(pair_ai)]
    extras = [r[...] for r in e_refs]
    o_ref[...] = epilogue(dots, extras, pl.program_id(1)).astype(o_ref.dtype)


def _matmul_fullk(a_list, pairs, extras, epilogue, *, m_rows, n_cols, out_dtype,
                  tm, tn, name, a_buffers=2):
    assert m_rows % tm == 0 and n_cols % tn == 0
    a_mode = {} if a_buffers == 2 else {"pipeline_mode": pl.Buffered(a_buffers)}
    in_specs = [pl.BlockSpec((tm, a.shape[1]), lambda i, j: (i, 0), **a_mode)
                for a in a_list]
    for ai, b, col_off in pairs:
        assert b.shape[0] == a_list[ai].shape[1]
        in_specs.append(pl.BlockSpec((b.shape[0], tn),
                                     lambda i, j, col_off=col_off: (0, j + col_off)))
    for _ in extras:
        in_specs.append(pl.BlockSpec((tm, tn), lambda i, j: (i, j)))
    body = functools.partial(_mm_fullk_body, tuple(ai for ai, _, _ in pairs),
                             len(a_list), len(extras), epilogue)
    return pl.pallas_call(
        body,
        grid=(m_rows // tm, n_cols // tn),
        in_specs=in_specs,
        out_specs=pl.BlockSpec((tm, tn), lambda i, j: (i, j)),
        out_shape=jax.ShapeDtypeStruct((m_rows, n_cols), out_dtype),
        compiler_params=pltpu.CompilerParams(
            dimension_semantics=("parallel", "arbitrary"),
            vmem_limit_bytes=VMEM_LIMIT),
        name=name,
    )(*a_list, *[b for _, b, _ in pairs], *extras)


def _mm_nt_body(shift, epilogue, tab_ref, a_ref, bm_ref, *rest):
    del tab_ref
    b = bm_ref[...]
    if shift:
        bn_ref, o_ref = rest
        b = jnp.concatenate([b[shift:], bn_ref[...]], axis=0)
    else:
        (o_ref,) = rest
    d = lax.dot_general(a_ref[...], b.astype(BF16), (((1,), (1,)), ((), ())),
                        preferred_element_type=F32)
    o_ref[...] = epilogue(d, pl.program_id(1)).astype(o_ref.dtype)


def _matmul_nt(a, bt, tile_starts, shift, epilogue, *, m_rows, out_dtype, tm, tn, name):
    assert m_rows % tm == 0 and all(s % tn == 0 for s in tile_starts)
    assert shift % 8 == 0 and (shift == 0 or tn % shift == 0)
    k_total = a.shape[1]
    assert bt.shape[1] == k_total
    tab = jnp.asarray([s // tn for s in tile_starts], jnp.int32)
    in_specs = [pl.BlockSpec((tm, k_total), lambda i, j, tab: (i, 0)),
                pl.BlockSpec((tn, k_total), lambda i, j, tab: (tab[j], 0))]
    operands = [tab, a, bt]
    if shift:
        per = tn // shift
        in_specs.append(pl.BlockSpec((shift, k_total),
                                     lambda i, j, tab: ((tab[j] + 1) * per, 0)))
        operands.append(bt)
    grid_spec = pltpu.PrefetchScalarGridSpec(
        num_scalar_prefetch=1,
        grid=(m_rows // tm, len(tile_starts)),
        in_specs=in_specs,
        out_specs=pl.BlockSpec((tm, tn), lambda i, j, tab: (i, j)),
    )
    return pl.pallas_call(
        functools.partial(_mm_nt_body, shift, epilogue),
        grid_spec=grid_spec,
        out_shape=jax.ShapeDtypeStruct((m_rows, tn * len(tile_starts)), out_dtype),
        compiler_params=pltpu.CompilerParams(
            dimension_semantics=("parallel", "arbitrary"),
            vmem_limit_bytes=VMEM_LIMIT),
        name=name,
    )(*operands)


def _mm_kacc_body(tk, k_total, scale, has_tail, a_ref, b_ref, r_ref, *rest):
    if has_tail:
        tail_ref, o_ref = rest
    else:
        (o_ref,) = rest
    k = pl.program_id(2)
    last_row_block = pl.program_id(0) == pl.num_programs(0) - 1

    def residual():
        r = r_ref[...]
        if has_tail:
            tm, t_rows = r.shape[0], tail_ref.shape[0]
            tail = jnp.concatenate(
                [jnp.zeros((tm - t_rows, r.shape[1]), r.dtype), tail_ref[...]], axis=0)
            row = lax.broadcasted_iota(jnp.int32, r.shape, 0)
            in_tail = jnp.logical_and(last_row_block, row >= tm - t_rows)
            r = jnp.where(in_tail, tail, r)
        return r

    def partial_product():
        a = a_ref[...]
        b = b_ref[...]
        if k_total % tk:
            limit = k_total - k * tk
            a = jnp.where(lax.broadcasted_iota(jnp.int32, a.shape, 1) < limit, a,
                          jnp.zeros_like(a))
            b = jnp.where(lax.broadcasted_iota(jnp.int32, b.shape, 0) < limit, b,
                          jnp.zeros_like(b))
        return scale * jnp.dot(a, b.astype(BF16), preferred_element_type=F32)

    @pl.when(k == 0)
    def _():
        o_ref[...] = residual() + partial_product()

    @pl.when(k > 0)
    def _():
        o_ref[...] += partial_product()


def _matmul_kacc(a, b, res, scale, *, m_rows, tm, tn, tk, name, res_tail=None):
    k_total, n_cols = b.shape
    assert a.shape[1] == k_total and m_rows % tm == 0 and n_cols % tn == 0
    in_specs = [pl.BlockSpec((tm, tk), lambda i, j, k: (i, k)),
                pl.BlockSpec((tk, tn), lambda i, j, k: (k, j)),
                pl.BlockSpec((tm, tn), lambda i, j, k: (i, j))]
    operands = [a, b, res]
    if res_tail is not None:
        assert res.shape[0] + res_tail.shape[0] == m_rows
        in_specs.append(pl.BlockSpec((res_tail.shape[0], tn), lambda i, j, k: (0, j)))
        operands.append(res_tail)
    return pl.pallas_call(
        functools.partial(_mm_kacc_body, tk, k_total, scale, res_tail is not None),
        grid=(m_rows // tm, n_cols // tn, pl.cdiv(k_total, tk)),
        in_specs=in_specs,
        out_specs=pl.BlockSpec((tm, tn), lambda i, j, k: (i, j)),
        out_shape=jax.ShapeDtypeStruct((m_rows, n_cols), F32),
        compiler_params=pltpu.CompilerParams(
            dimension_semantics=("parallel", "parallel", "arbitrary"),
            vmem_limit_bytes=VMEM_LIMIT),
        name=name,
    )(*operands)


def _swiglu_ffn(n, h_res, w1, w3, w2, *, m_rows, tm_up, tm_down, res_tail=None):
    def up_epilogue(dots, extras, j):
        a1, a3 = dots
        return a1 * _sigmoid(a1) * a3

    u = _matmul_fullk([n], [(0, w1, 0), (0, w3, 0)], [], up_epilogue,
                      m_rows=m_rows, n_cols=w1.shape[1], out_dtype=BF16,
                      tm=tm_up, tn=256, name="ffn_up", a_buffers=1)
    return _matmul_kacc(u, w2, h_res, 0.5, m_rows=m_rows, tm=tm_down, tn=1024, tk=1024,
                        name="ffn_down", res_tail=res_tail)


def _small_proj_body(f_off, ab_off, n_ref, wf_ref, fb_ref, wab_ref, kaug_ref, ab_ref,
                     carry_ref):
    t = pl.program_id(0)
    is_meta = t == 0
    nt_dims = (((1,), (1,)), ((), ()))

    @pl.when(is_meta)
    def _():
        carry_ref[...] = jnp.zeros_like(carry_ref)

    n = n_ref[...]
    ft = lax.dot_general(n, wf_ref[...].astype(BF16), nt_dims,
                         preferred_element_type=F32)
    row = lax.broadcasted_iota(jnp.int32, ft.shape, 0)
    col = lax.broadcasted_iota(jnp.int32, ft.shape, 1)
    ft = jnp.where(jnp.logical_and(col >= f_off, col < f_off + FOX_HEADS), ft, 0.0)
    shifts = [(p * FOX_HEADS - f_off) % LANE for p in range(3)]
    ft = sum(pltpu.roll(ft, s, 1) if s else ft for s in shifts)
    logf = _log_sigmoid(ft + fb_ref[...])
    valid = jnp.logical_or(jnp.logical_not(is_meta), row < N_META)
    logf = jnp.where(valid, logf, 0.0)
    tsz = logf.shape[0]
    lower = jnp.where(lax.broadcasted_iota(jnp.int32, (tsz, tsz), 1)
                      <= lax.broadcasted_iota(jnp.int32, (tsz, tsz), 0),
                      1.0, 0.0).astype(BF16)
    c = carry_ref[0:1, :] + _tri_matmul(lower, logf)
    carry_ref[...] = carry_ref[...] + jnp.sum(logf, axis=0, keepdims=True)

    bias = -LOG2E * c
    hi = bias.astype(BF16).astype(F32)
    mid = (bias - hi).astype(BF16).astype(F32)
    lo = (bias - hi - mid).astype(BF16).astype(F32)
    pieces = jnp.where(col < FOX_HEADS, hi,
                       jnp.where(col < 2 * FOX_HEADS, mid,
                                 jnp.where(col < 3 * FOX_HEADS, lo, 0.0)))
    removed = jnp.where(col < FOX_HEADS, NEG_BIG, 0.0)
    kaug_ref[...] = jnp.where(valid, pieces, removed).astype(BF16)
    wrow = lax.broadcasted_iota(jnp.int32, wab_ref.shape, 0)
    wab = jnp.where(jnp.logical_and(wrow >= ab_off, wrow < ab_off + GLA_GATE_RANK),
                    wab_ref[...], 0.0)
    ab_ref[...] = lax.dot_general(n, wab.astype(BF16), nt_dims,
                                  preferred_element_type=F32)


def _small_proj(n, wt, f_bias, *, seq, f_col, ab_col):
    lp, d = n.shape
    heads = f_bias.shape[0]
    assert heads == FOX_HEADS and 3 * heads <= LANE
    assert f_col % LANE + heads <= LANE and ab_col % LANE + GLA_GATE_RANK <= LANE
    tsz = 4 * META_BLOCK
    assert seq % tsz == 0
    n_x = seq // tsz
    fb3 = jnp.concatenate([f_bias, f_bias, f_bias,
                           jnp.zeros((LANE - 3 * heads,), f_bias.dtype)])

    def row_block(t):
        return jnp.where(t == 0, n_x, t - 1)

    return pl.pallas_call(
        functools.partial(_small_proj_body, f_col % LANE, ab_col % LANE),
        grid=(n_x + 1,),
        in_specs=[pl.BlockSpec((tsz, d), lambda t: (row_block(t), 0)),
                  pl.BlockSpec((LANE, d), lambda t: (f_col // LANE, 0)),
                  pl.BlockSpec((1, LANE), lambda t: (0, 0)),
                  pl.BlockSpec((LANE, d), lambda t: (ab_col // LANE, 0))],
        out_specs=[pl.BlockSpec((tsz, LANE), lambda t: (row_block(t), 0)),
                   pl.BlockSpec((tsz, LANE), lambda t: (row_block(t), 0))],
        out_shape=[jax.ShapeDtypeStruct((lp, LANE), BF16),
                   jax.ShapeDtypeStruct((lp, LANE), F32)],
        scratch_shapes=[pltpu.VMEM((8, LANE), F32)],
        compiler_params=pltpu.CompilerParams(
            dimension_semantics=("arbitrary",), vmem_limit_bytes=VMEM_LIMIT),
        name="small_proj",
    )(n, wt, fb3.reshape(1, LANE), wt)


def _fox_body(tq, rq, hb, i_tab, j_tab, q_ref, km_ref, vm_ref, am_ref, kx_ref, vx_ref,
              ax_ref, o_ref, m_ref, acc_ref):
    g = pl.program_id(0)
    t = pl.program_id(1)
    i = i_tab[t]
    j = j_tab[t]
    dh = LANE
    lane = lax.broadcasted_iota(jnp.int32, (rq, LANE), 1)

    def q_ones(hh):
        h = g * hb + hh
        pick = jnp.logical_or(lane == h, jnp.logical_or(lane == h + FOX_HEADS,
                                                        lane == h + 2 * FOX_HEADS))
        return jnp.where(pick, 1.0, 0.0).astype(BF16)

    def online_update(with_meta, diagonal):
        n_m = km_ref.shape[0] if with_meta else 0
        n_k = n_m + kx_ref.shape[0]

        def rows_of(m_ref_, x_ref_, c):
            x_part = x_ref_[:, c]
            return jnp.concatenate([m_ref_[:, c], x_part], axis=0) if with_meta else x_part

        ones = jnp.ones((n_k, LANE), BF16)
        aug = rows_of(am_ref, ax_ref, slice(None))
        cols_of = [slice(hh * dh, (hh + 1) * dh) for hh in range(hb)]
        keys_of = [jnp.concatenate([rows_of(km_ref, kx_ref, c), aug], axis=1)
                   for c in cols_of]
        vals_of = [jnp.concatenate([rows_of(vm_ref, vx_ref, c), ones], axis=1)
                   for c in cols_of]
        pick_of = [q_ones(hh) for hh in range(hb)]
        for r in range(tq // rq):
            for hh in range(hb):
                cols, keys, vals, pick = cols_of[hh], keys_of[hh], vals_of[hh], pick_of[hh]
                rows = pl.ds(r * rq, rq)
                n_c = n_m + (r + 1) * rq if diagonal else n_k
                q = jnp.concatenate([q_ref[rows, cols], pick], axis=1)
                s = lax.dot_general(q, keys[:n_c], (((1,), (1,)), ((), ())),
                                    preferred_element_type=F32)
                if diagonal:
                    tri = (lax.broadcasted_iota(jnp.int32, (rq, rq), 1)
                           <= lax.broadcasted_iota(jnp.int32, (rq, rq), 0))
                    s_last = jnp.where(tri, s[:, n_c - rq:], NEG_BIG)
                    s = s_last if n_c == rq else jnp.concatenate(
                        [s[:, :n_c - rq], s_last], axis=1)
                m_prev = m_ref[hh, rows, :]
                m_next = jnp.maximum(m_prev, jnp.max(s, axis=1, keepdims=True))
                p = jnp.exp2(s - jnp.tile(m_next, (1, n_c // LANE)))
                alpha = jnp.exp2(m_prev - m_next)
                m_ref[hh, rows, :] = m_next
                acc_ref[hh, rows, :] = (jnp.tile(alpha, (1, 2)) * acc_ref[hh, rows, :]
                                        + jnp.dot(p.astype(BF16), vals[:n_c],
                                                  preferred_element_type=F32))

    def start():
        m_ref[...] = jnp.full_like(m_ref, NEG_BIG)
        acc_ref[...] = jnp.zeros_like(acc_ref)

    def finish():
        for hh in range(hb):
            acc = acc_ref[hh]
            o_ref[:, hh * dh:(hh + 1) * dh] = (acc[:, :dh] / acc[:, dh:]).astype(
                o_ref.dtype)

    @pl.when(jnp.logical_and(j == 0, i == 0))
    def _():
        start()
        online_update(True, True)
        finish()

    @pl.when(jnp.logical_and(j == 0, i > 0))
    def _():
        start()
        online_update(True, False)

    @pl.when(jnp.logical_and(j > 0, j < i))
    def _():
        online_update(False, False)

    @pl.when(jnp.logical_and(j > 0, j == i))
    def _():
        online_update(False, True)
        finish()


def _fox_attention(qkv, kaug, *, seq, tq, rq, hb):
    heads = FOX_HEADS
    dh = FOX_HEAD_DIM
    assert dh == LANE and seq % tq == 0 and tq % rq == 0 and rq % LANE == 0
    assert heads % hb == 0
    n_q = seq // tq
    n_g = heads // hb
    meta_blk = seq // META_BLOCK
    steps = [(i, j) for i in range(n_q) for j in range(i + 1)]
    i_tab = jnp.asarray([s[0] for s in steps], jnp.int32)
    j_tab = jnp.asarray([s[1] for s in steps], jnp.int32)
    wb = hb * dh

    def kv_block(t, j_tab):
        return j_tab[t]

    grid_spec = pltpu.PrefetchScalarGridSpec(
        num_scalar_prefetch=2,
        grid=(n_g, len(steps)),
        in_specs=[
            pl.BlockSpec((tq, wb), lambda g, t, it, jt: (it[t], g)),
            pl.BlockSpec((META_BLOCK, wb), lambda g, t, it, jt: (meta_blk, n_g + g)),
            pl.BlockSpec((META_BLOCK, wb),
                         lambda g, t, it, jt: (meta_blk, 2 * n_g + g)),
            pl.BlockSpec((META_BLOCK, LANE), lambda g, t, it, jt: (meta_blk, 0)),
            pl.BlockSpec((tq, wb), lambda g, t, it, jt: (kv_block(t, jt), n_g + g)),
            pl.BlockSpec((tq, wb),
                         lambda g, t, it, jt: (kv_block(t, jt), 2 * n_g + g)),
            pl.BlockSpec((tq, LANE), lambda g, t, it, jt: (kv_block(t, jt), 0)),
        ],
        out_specs=pl.BlockSpec((tq, wb), lambda g, t, it, jt: (it[t], g)),
        scratch_shapes=[pltpu.VMEM((hb, tq, LANE), F32),
                        pltpu.VMEM((hb, tq, 2 * dh), F32)],
    )
    return pl.pallas_call(
        functools.partial(_fox_body, tq, rq, hb),
        grid_spec=grid_spec,
        out_shape=jax.ShapeDtypeStruct((seq, heads * dh), BF16),
        compiler_params=pltpu.CompilerParams(
            dimension_semantics=("parallel", "arbitrary"),
            vmem_limit_bytes=VMEM_LIMIT),
        name="fox_attention",
    )(i_tab, j_tab, qkv, qkv, qkv, kaug, qkv, qkv, kaug)


def _gla_body(n_chunks, qv_ref, kr_ref, ab_ref, w2_ref, b_ref, g_ref, o_ref, state_ref):
    t = pl.program_id(0)
    is_meta = t == 0

    @pl.when(is_meta)
    def _():
        state_ref[...] = jnp.zeros_like(state_ref)

    z = jnp.dot(ab_ref[...].astype(BF16), w2_ref[...].astype(BF16),
                preferred_element_type=F32) + b_ref[...]
    la = _log_sigmoid(z) / GLA_GATE_TAU
    n_rows = la.shape[0]
    row = lax.broadcasted_iota(jnp.int32, la.shape, 0)
    la = jnp.where(jnp.logical_and(is_meta, row >= N_META), 0.0, la)
    ri = lax.broadcasted_iota(jnp.int32, (n_rows, n_rows), 0)
    ci = lax.broadcasted_iota(jnp.int32, (n_rows, n_rows), 1)
    same_chunk = (ri // CHUNK) == (ci // CHUNK)
    lower = jnp.where(jnp.logical_and(same_chunk, ci <= ri), 1.0, 0.0).astype(BF16)
    cum = _tri_matmul(lower, la)
    totals = [cum[(c + 1) * CHUNK - 1:(c + 1) * CHUNK, :] for c in range(n_chunks)]
    total_rows = jnp.concatenate(
        [jnp.broadcast_to(tot, (CHUNK, tot.shape[1])) for tot in totals], axis=0)
    k_dec = (kr_ref[:, :GLA_KW] * jnp.exp(total_rows - cum)).astype(BF16)

    for c in range(n_chunks):
        rows = slice(c * CHUNK, (c + 1) * CHUNK)
        decay = jnp.exp(totals[c])
        for h in range(GLA_HEADS):
            ks = slice(h * GLA_DK, (h + 1) * GLA_DK)
            vs = slice(GLA_KW + h * GLA_DV, GLA_KW + (h + 1) * GLA_DV)
            kv = lax.dot_general(qv_ref[rows, vs], k_dec[rows, ks],
                                 (((0,), (0,)), ((), ())),
                                 preferred_element_type=F32)
            s_new = state_ref[h] * decay[:, ks] + kv
            state_ref[h] = s_new
            o = lax.dot_general(qv_ref[rows, ks], s_new.astype(BF16),
                                (((1,), (1,)), ((), ())),
                                preferred_element_type=F32)
            ms = jnp.mean(o * o, axis=-1, keepdims=True)
            r = kr_ref[rows, vs]
            o_ref[rows, h * GLA_DV:(h + 1) * GLA_DV] = (
                o * lax.rsqrt(ms + EPS) * g_ref[...] * (r * _sigmoid(r))
            ).astype(o_ref.dtype)


def _gla(qv, kr, ab, w2, b, g, *, seq, ab_off):
    rows = CHUNK
    assert rows % CHUNK == 0 and seq % rows == 0
    n_x = seq // rows
    w2 = jnp.zeros((LANE, GLA_KW), w2.dtype).at[ab_off:ab_off + GLA_GATE_RANK].set(w2)

    def row_block(t):
        return jnp.where(t == 0, n_x, t - 1)

    width = GLA_KW + GLA_VW
    return pl.pallas_call(
        functools.partial(_gla_body, rows // CHUNK),
        grid=(n_x + 1,),
        in_specs=[pl.BlockSpec((rows, width), lambda t: (row_block(t), 0)),
                  pl.BlockSpec((rows, width), lambda t: (row_block(t), 0)),
                  pl.BlockSpec((rows, LANE), lambda t: (row_block(t), 0)),
                  pl.BlockSpec((LANE, GLA_KW), lambda t: (0, 0)),
                  pl.BlockSpec((1, GLA_KW), lambda t: (0, 0)),
                  pl.BlockSpec((1, GLA_DV), lambda t: (0, 0))],
        out_specs=pl.BlockSpec((rows, GLA_VW), lambda t: (jnp.maximum(t - 1, 0), 0)),
        out_shape=jax.ShapeDtypeStruct((seq, GLA_VW), BF16),
        scratch_shapes=[pltpu.VMEM((GLA_HEADS, GLA_DV, GLA_DK), F32)],
        compiler_params=pltpu.CompilerParams(
            dimension_semantics=("arbitrary",), vmem_limit_bytes=VMEM_LIMIT),
        name="gla",
    )(qv, kr, ab, w2, b.reshape(1, GLA_KW), g.reshape(1, GLA_DV))


def kernel(x, meta_tokens, norm1_g, ffn1_w1, ffn1_w3, ffn1_w2, norm_mix_g, w_in,
           fox_f_bias, gla_alpha_w2, gla_alpha_b, gla_norm_g, w_gate, w_proj_fox,
           w_proj_gla, w_out, norm2_g, ffn2_w1, ffn2_w3, ffn2_w2, norm_final_g):
    batch, seq, d = x.shape
    assert batch == 1 and seq % 1024 == 0
    depth = norm1_g.shape[0]
    lp = seq + META_BLOCK
    assert lp % 8 == 0 and (lp // 8) % 16 == 0
    tm_all = lp // 8
    tm_x = seq // 8

    norm_tile = 2 * META_BLOCK
    meta_pad = jnp.concatenate(
        [meta_tokens.astype(x.dtype), jnp.zeros((norm_tile - N_META, d), x.dtype)], axis=0)
    x2 = x[0]

    o_fa = FOX_WIDTH * 3
    o_qb = o_fa + FOX_HEADS
    o_kb = o_qb + GLA_KW
    o_vb = o_kb + GLA_KW
    o_rb = o_vb + GLA_VW
    o_ab = o_rb + GLA_VW
    fox_scale = FOX_HEAD_DIM ** -0.5 * LOG2E
    gla_scale = GLA_DK ** -0.5

    assert depth == 1
    for li in range(depth):
        n1 = _rmsnorm_frames_meta(x2, meta_pad, norm1_g[li], lp=lp)
        h = _swiglu_ffn(n1, x2, ffn1_w1[li], ffn1_w3[li], ffn1_w2[li], m_rows=lp,
                        tm_up=2 * tm_all, tm_down=2 * tm_all,
                        res_tail=meta_pad[:META_BLOCK])

        n = _rmsnorm(h, norm_mix_g[li], lp, BF16)
        wt = jnp.transpose(w_in[li])

        tn_p = 512
        shift = o_qb % tn_p

        def tiles(start, width):
            assert start % tn_p in (0, shift) and width % tn_p == 0
            return [start - start % tn_p + r for r in range(0, width, tn_p)]

        def fox_epilogue(dot, j):
            return dot * jnp.where(j < FOX_WIDTH // tn_p, fox_scale, 1.0)

        qkv_a = _matmul_nt(n, wt, tiles(0, 3 * FOX_WIDTH), 0, fox_epilogue, m_rows=lp,
                           out_dtype=BF16, tm=tm_all, tn=tn_p, name="proj_fox")

        def gla_q_epilogue(dot, j):
            return dot * jnp.where(j < GLA_KW // tn_p, gla_scale, 1.0)

        qv_b = _matmul_nt(n, wt, tiles(o_qb, GLA_KW) + tiles(o_vb, GLA_VW), shift,
                          gla_q_epilogue, m_rows=lp, out_dtype=BF16,
                          tm=tm_all, tn=tn_p, name="proj_gla_qv")
        kr_b = _matmul_nt(n, wt, tiles(o_kb, GLA_KW) + tiles(o_rb, GLA_VW), shift,
                          lambda dot, j: dot, m_rows=lp, out_dtype=F32,
                          tm=tm_all, tn=tn_p, name="proj_gla_kr")

        kaug, ab = _small_proj(n, wt, fox_f_bias[li], seq=seq, f_col=o_fa, ab_col=o_ab)
        o_a = _fox_attention(qkv_a, kaug, seq=seq, tq=2048, rq=256, hb=2)
        o_b = _gla(qv_b, kr_b, ab, gla_alpha_w2[li], gla_alpha_b[li], gla_norm_g[li],
                   seq=seq, ab_off=o_ab % LANE)

        wg = w_gate[li]

        def mix_epilogue(accs, extras, j):
            ga, gb, pa, pb = accs
            return _sigmoid(ga) * pa + _sigmoid(gb) * pb

        tn_m = 256
        y = _matmul_fullk(
            [n, o_a, o_b],
            [(0, wg, 0), (0, wg, d // tn_m), (1, w_proj_fox[li], 0),
             (2, w_proj_gla[li], 0)],
            [], mix_epilogue, m_rows=seq, n_cols=d, out_dtype=BF16,
            tm=tm_x, tn=tn_m, name="mix", a_buffers=1)

        h = _matmul_fullk([y], [(0, w_out[li], 0)], [h],
                          lambda a, e, j: e[0] + a[0], m_rows=seq, n_cols=d,
                          out_dtype=F32, tm=tm_x, tn=512, name="out_proj")

        n2 = _rmsnorm(h, norm2_g[li], seq, BF16)
        h = _swiglu_ffn(n2, h, ffn2_w1[li], ffn2_w3[li], ffn2_w2[li], m_rows=seq,
                        tm_up=2 * tm_x, tm_down=2 * tm_x)

    out = _rmsnorm(h, norm_final_g, seq, F32)
    return out.reshape(batch, seq, d)
```

```python
import functools

import jax
import jax.numpy as jnp
from jax import lax
from jax.experimental import pallas as pl
from jax.experimental.pallas import tpu as pltpu

F32 = jnp.float32
BF16 = jnp.bfloat16

EPS = 1e-6
N_META = 16
CHUNK = 64
FOX_HEADS = 16
FOX_HEAD_DIM = 128
FOX_WIDTH = FOX_HEADS * FOX_HEAD_DIM
GLA_HEADS = 4
GLA_DK = 256
GLA_DV = 512
GLA_KW = GLA_HEADS * GLA_DK
GLA_VW = GLA_HEADS * GLA_DV
GLA_GATE_RANK = 16
GLA_GATE_TAU = 16.0

LANE = 128
META_BLOCK = LANE
NEG_BIG = -1e30
LOG2E = 1.4426950408889634
VMEM_LIMIT = 60 * 1024 * 1024


def _log_sigmoid(z):
    return jnp.minimum(z, 0.0) - jnp.log(1.0 + jnp.exp(-jnp.abs(z)))


def _sigmoid(z):
    return 1.0 / (1.0 + jnp.exp(-z))


def _tri_matmul(tri, x):
    hi = x.astype(BF16)
    rest = x - hi.astype(F32)
    mid = rest.astype(BF16)
    lo = (rest - mid.astype(F32)).astype(BF16)
    return (jnp.dot(tri, hi, preferred_element_type=F32)
            + jnp.dot(tri, mid, preferred_element_type=F32)
            + jnp.dot(tri, lo, preferred_element_type=F32))


def _rmsnorm_body(x_ref, g_ref, o_ref):
    x = x_ref[...]
    ms = jnp.mean(x * x, axis=-1, keepdims=True)
    o_ref[...] = (x * lax.rsqrt(ms + EPS) * g_ref[...]).astype(o_ref.dtype)


def _rmsnorm(x, g, rows, out_dtype):
    d = x.shape[1]
    tile = max(t for t in range(16, 513, 16) if rows % t == 0)
    return pl.pallas_call(
        _rmsnorm_body,
        grid=(rows // tile,),
        in_specs=[pl.BlockSpec((tile, d), lambda i: (i, 0)),
                  pl.BlockSpec((1, d), lambda i: (0, 0))],
        out_specs=pl.BlockSpec((tile, d), lambda i: (i, 0)),
        out_shape=jax.ShapeDtypeStruct((rows, d), out_dtype),
        compiler_params=pltpu.CompilerParams(
            dimension_semantics=("parallel",), vmem_limit_bytes=VMEM_LIMIT),
        name="rmsnorm",
    )(x, g.reshape(1, d))


def _rmsnorm_frames_meta_body(n_x, x_ref, mp_ref, g_ref, o_ref):
    src = jnp.where(pl.program_id(0) < n_x, x_ref[...], mp_ref[...])
    ms = jnp.mean(src * src, axis=-1, keepdims=True)
    o_ref[...] = (src * lax.rsqrt(ms + EPS) * g_ref[...]).astype(o_ref.dtype)


def _rmsnorm_frames_meta(x, meta_pad, g, *, lp):
    seq, d = x.shape
    tile = meta_pad.shape[0]
    assert seq % tile == 0 and seq < lp <= seq + tile
    n_x = seq // tile
    return pl.pallas_call(
        functools.partial(_rmsnorm_frames_meta_body, n_x),
        grid=(n_x + 1,),
        in_specs=[pl.BlockSpec((tile, d), lambda i: (jnp.minimum(i, n_x - 1), 0)),
                  pl.BlockSpec((tile, d), lambda i: (0, 0)),
                  pl.BlockSpec((1, d), lambda i: (0, 0))],
        out_specs=pl.BlockSpec((tile, d), lambda i: (i, 0)),
        out_shape=jax.ShapeDtypeStruct((lp, d), BF16),
        compiler_params=pltpu.CompilerParams(
            dimension_semantics=("parallel",), vmem_limit_bytes=VMEM_LIMIT),
        name="rmsnorm_in",
    )(x, meta_pad, g.reshape(1, d))


def _mm_fullk_body(pair_ai, n_a, n_extra, epilogue, *refs):
    n_p = len(pair_ai)
    a_refs = refs[:n_a]
    b_refs = refs[n_a:n_a + n_p]
    e_refs = refs[n_a + n_p:n_a + n_p + n_extra]
    o_ref = refs[n_a + n_p + n_extra]
    dots = [jnp.dot(a_refs[ai][...], b_refs[p][...].astype(BF16),
                    preferred_element_type=F32) for p, ai in enumerate(pair_ai)]
    extras = [r[...] for r in e_refs]
    o_ref[...] = epilogue(dots, extras, pl.program_id(1)).astype(o_ref.dtype)


def _matmul_fullk(a_list, pairs, extras, epilogue, *, m_rows, n_cols, out_dtype,
                  tm, tn, name, a_buffers=2):
    assert m_rows % tm == 0 and n_cols % tn == 0
    a_mode = {} if a_buffers == 2 else {"pipeline_mode": pl.Buffered(a_buffers)}
    in_specs = [pl.BlockSpec((tm, a.shape[1]), lambda i, j: (i, 0), **a_mode)
                for a in a_list]
    for ai, b, col_off in pairs:
        assert b.shape[0] == a_list[ai].shape[1]
        in_specs.append(pl.BlockSpec((b.shape[0], tn),
                                     lambda i, j, col_off=col_off: (0, j + col_off)))
    for _ in extras:
        in_specs.append(pl.BlockSpec((tm, tn), lambda i, j: (i, j)))
    body = functools.partial(_mm_fullk_body, tuple(ai for ai, _, _ in pairs),
                             len(a_list), len(extras), epilogue)
    return pl.pallas_call(
        body,
        grid=(m_rows // tm, n_cols // tn),
        in_specs=in_specs,
        out_specs=pl.BlockSpec((tm, tn), lambda i, j: (i, j)),
        out_shape=jax.ShapeDtypeStruct((m_rows, n_cols), out_dtype),
        compiler_params=pltpu.CompilerParams(
            dimension_semantics=("parallel", "arbitrary"),
            vmem_limit_bytes=VMEM_LIMIT),
        name=name,
    )(*a_list, *[b for _, b, _ in pairs], *extras)


def _mm_nt_body(shift, epilogue, tab_ref, a_ref, bm_ref, *rest):
    del tab_ref
    b = bm_ref[...]
    if shift:
        bn_ref, o_ref = rest
        b = jnp.concatenate([b[shift:], bn_ref[...]], axis=0)
    else:
        (o_ref,) = rest
    d = lax.dot_general(a_ref[...], b.astype(BF16), (((1,), (1,)), ((), ())),
                        preferred_element_type=F32)
    o_ref[...] = epilogue(d, pl.program_id(1)).astype(o_ref.dtype)


def _matmul_nt(a, bt, tile_starts, shift, epilogue, *, m_rows, out_dtype, tm, tn, name):
    assert m_rows % tm == 0 and all(s % tn == 0 for s in tile_starts)
    assert shift % 8 == 0 and (shift == 0 or tn % shift == 0)
    k_total = a.shape[1]
    assert bt.shape[1] == k_total
    tab = jnp.asarray([s // tn for s in tile_starts], jnp.int32)
    in_specs = [pl.BlockSpec((tm, k_total), lambda i, j, tab: (i, 0)),
                pl.BlockSpec((tn, k_total), lambda i, j, tab: (tab[j], 0))]
    operands = [tab, a, bt]
    if shift:
        per = tn // shift
        in_specs.append(pl.BlockSpec((shift, k_total),
                                     lambda i, j, tab: ((tab[j] + 1) * per, 0)))
        operands.append(bt)
    grid_spec = pltpu.PrefetchScalarGridSpec(
        num_scalar_prefetch=1,
        grid=(m_rows // tm, len(tile_starts)),
        in_specs=in_specs,
        out_specs=pl.BlockSpec((tm, tn), lambda i, j, tab: (i, j)),
    )
    return pl.pallas_call(
        functools.partial(_mm_nt_body, shift, epilogue),
        grid_spec=grid_spec,
        out_shape=jax.ShapeDtypeStruct((m_rows, tn * len(tile_starts)), out_dtype),
        compiler_params=pltpu.CompilerParams(
            dimension_semantics=("parallel", "arbitrary"),
            vmem_limit_bytes=VMEM_LIMIT),
        name=name,
    )(*operands)


def _mm_kacc_body(tk, k_total, scale, has_tail, a_ref, b_ref, r_ref, *rest):
    if has_tail:
        tail_ref, o_ref = rest
    else:
        (o_ref,) = rest
    k = pl.program_id(2)
    last_row_block = pl.program_id(0) == pl.num_programs(0) - 1

    def residual():
        r = r_ref[...]
        if has_tail:
            tm, t_rows = r.shape[0], tail_ref.shape[0]
            tail = jnp.concatenate(
                [jnp.zeros((tm - t_rows, r.shape[1]), r.dtype), tail_ref[...]], axis=0)
            row = lax.broadcasted_iota(jnp.int32, r.shape, 0)
            in_tail = jnp.logical_and(last_row_block, row >= tm - t_rows)
            r = jnp.where(in_tail, tail, r)
        return r

    def partial_product():
        a = a_ref[...]
        b = b_ref[...]
        if k_total % tk:
            limit = k_total - k * tk
            a = jnp.where(lax.broadcasted_iota(jnp.int32, a.shape, 1) < limit, a,
                          jnp.zeros_like(a))
            b = jnp.where(lax.broadcasted_iota(jnp.int32, b.shape, 0) < limit, b,
                          jnp.zeros_like(b))
        return scale * jnp.dot(a, b.astype(BF16), preferred_element_type=F32)

    @pl.when(k == 0)
    def _():
        o_ref[...] = residual() + partial_product()

    @pl.when(k > 0)
    def _():
        o_ref[...] += partial_product()


def _matmul_kacc(a, b, res, scale, *, m_rows, tm, tn, tk, name, res_tail=None):
    k_total, n_cols = b.shape
    assert a.shape[1] == k_total and m_rows % tm == 0 and n_cols % tn == 0
    in_specs = [pl.BlockSpec((tm, tk), lambda i, j, k: (i, k)),
                pl.BlockSpec((tk, tn), lambda i, j, k: (k, j)),
                pl.BlockSpec((tm, tn), lambda i, j, k: (i, j))]
    operands = [a, b, res]
    if res_tail is not None:
        assert res.shape[0] + res_tail.shape[0] == m_rows
        in_specs.append(pl.BlockSpec((res_tail.shape[0], tn), lambda i, j, k: (0, j)))
        operands.append(res_tail)
    return pl.pallas_call(
        functools.partial(_mm_kacc_body, tk, k_total, scale, res_tail is not None),
        grid=(m_rows // tm, n_cols // tn, pl.cdiv(k_total, tk)),
        in_specs=in_specs,
        out_specs=pl.BlockSpec((tm, tn), lambda i, j, k: (i, j)),
        out_shape=jax.ShapeDtypeStruct((m_rows, n_cols), F32),
        compiler_params=pltpu.CompilerParams(
            dimension_semantics=("parallel", "parallel", "arbitrary"),
            vmem_limit_bytes=VMEM_LIMIT),
        name=name,
    )(*operands)


def _swiglu_ffn(n, h_res, w1, w3, w2, *, m_rows, tm_up, tm_down, res_tail=None):
    def up_epilogue(dots, extras, j):
        a1, a3 = dots
        return a1 * _sigmoid(a1) * a3

    u = _matmul_fullk([n], [(0, w1, 0), (0, w3, 0)], [], up_epilogue,
                      m_rows=m_rows, n_cols=w1.shape[1], out_dtype=BF16,
                      tm=tm_up, tn=256, name="ffn_up", a_buffers=1)
    return _matmul_kacc(u, w2, h_res, 0.5, m_rows=m_rows, tm=tm_down, tn=1024, tk=1024,
                        name="ffn_down", res_tail=res_tail)


def _small_proj_body(f_off, ab_off, n_ref, wf_ref, fb_ref, wab_ref, kaug_ref, ab_ref,
                     carry_ref):
    t = pl.program_id(0)
    is_meta = t == 0
    nt_dims = (((1,), (1,)), ((), ()))

    @pl.when(is_meta)
    def _():
        carry_ref[...] = jnp.zeros_like(carry_ref)

    n = n_ref[...]
    ft = lax.dot_general(n, wf_ref[...].astype(BF16), nt_dims,
                         preferred_element_type=F32)
    row = lax.broadcasted_iota(jnp.int32, ft.shape, 0)
    col = lax.broadcasted_iota(jnp.int32, ft.shape, 1)
    ft = jnp.where(jnp.logical_and(col >= f_off, col < f_off + FOX_HEADS), ft, 0.0)
    shifts = [(p * FOX_HEADS - f_off) % LANE for p in range(3)]
    ft = sum(pltpu.roll(ft, s, 1) if s else ft for s in shifts)
    logf = _log_sigmoid(ft + fb_ref[...])
    valid = jnp.logical_or(jnp.logical_not(is_meta), row < N_META)
    logf = jnp.where(valid, logf, 0.0)
    tsz = logf.shape[0]
    lower = jnp.where(lax.broadcasted_iota(jnp.int32, (tsz, tsz), 1)
                      <= lax.broadcasted_iota(jnp.int32, (tsz, tsz), 0),
                      1.0, 0.0).astype(BF16)
    c = carry_ref[0:1, :] + _tri_matmul(lower, logf)
    carry_ref[...] = carry_ref[...] + jnp.sum(logf, axis=0, keepdims=True)

    bias = -LOG2E * c
    hi = bias.astype(BF16).astype(F32)
    mid = (bias - hi).astype(BF16).astype(F32)
    lo = (bias - hi - mid).astype(BF16).astype(F32)
    pieces = jnp.where(col < FOX_HEADS, hi,
                       jnp.where(col < 2 * FOX_HEADS, mid,
                                 jnp.where(col < 3 * FOX_HEADS, lo, 0.0)))
    removed = jnp.where(col < FOX_HEADS, NEG_BIG, 0.0)
    kaug_ref[...] = jnp.where(valid, pieces, removed).astype(BF16)
    wrow = lax.broadcasted_iota(jnp.int32, wab_ref.shape, 0)
    wab = jnp.where(jnp.logical_and(wrow >= ab_off, wrow < ab_off + GLA_GATE_RANK),
                    wab_ref[...], 0.0)
    ab_ref[...] = lax.dot_general(n, wab.astype(BF16), nt_dims,
                                  preferred_element_type=F32)


def _small_proj(n, wt, f_bias, *, seq, f_col, ab_col):
    lp, d = n.shape
    heads = f_bias.shape[0]
    assert heads == FOX_HEADS and 3 * heads <= LANE
    assert f_col % LANE + heads <= LANE and ab_col % LANE + GLA_GATE_RANK <= LANE
    tsz = 4 * META_BLOCK
    assert seq % tsz == 0
    n_x = seq // tsz
    fb3 = jnp.concatenate([f_bias, f_bias, f_bias,
                           jnp.zeros((LANE - 3 * heads,), f_bias.dtype)])

    def row_block(t):
        return jnp.where(t == 0, n_x, t - 1)

    return pl.pallas_call(
        functools.partial(_small_proj_body, f_col % LANE, ab_col % LANE),
        grid=(n_x + 1,),
        in_specs=[pl.BlockSpec((tsz, d), lambda t: (row_block(t), 0)),
                  pl.BlockSpec((LANE, d), lambda t: (f_col // LANE, 0)),
                  pl.BlockSpec((1, LANE), lambda t: (0, 0)),
                  pl.BlockSpec((LANE, d), lambda t: (ab_col // LANE, 0))],
        out_specs=[pl.BlockSpec((tsz, LANE), lambda t: (row_block(t), 0)),
                   pl.BlockSpec((tsz, LANE), lambda t: (row_block(t), 0))],
        out_shape=[jax.ShapeDtypeStruct((lp, LANE), BF16),
                   jax.ShapeDtypeStruct((lp, LANE), F32)],
        scratch_shapes=[pltpu.VMEM((8, LANE), F32)],
        compiler_params=pltpu.CompilerParams(
            dimension_semantics=("arbitrary",), vmem_limit_bytes=VMEM_LIMIT),
        name="small_proj",
    )(n, wt, fb3.reshape(1, LANE), wt)


def _fox_body(tq, rq, hb, i_tab, j_tab, q_ref, km_ref, vm_ref, am_ref, kx_ref, vx_ref,
              ax_ref, o_ref, m_ref, acc_ref):
    g = pl.program_id(0)
    t = pl.program_id(1)
    i = i_tab[t]
    j = j_tab[t]
    dh = LANE
    lane = lax.broadcasted_iota(jnp.int32, (rq, LANE), 1)

    def q_ones(hh):
        h = g * hb + hh
        pick = jnp.logical_or(lane == h, jnp.logical_or(lane == h + FOX_HEADS,
                                                        lane == h + 2 * FOX_HEADS))
        return jnp.where(pick, 1.0, 0.0).astype(BF16)

    def online_update(k_ref, v_ref, a_ref, diagonal):
        n_k = k_ref.shape[0]
        ones = jnp.ones((n_k, LANE), BF16)
        aug = a_ref[...]
        cols_of = [slice(hh * dh, (hh + 1) * dh) for hh in range(hb)]
        keys_of = [jnp.concatenate([k_ref[:, c], aug], axis=1) for c in cols_of]
        vals_of = [jnp.concatenate([v_ref[:, c], ones], axis=1) for c in cols_of]
        pick_of = [q_ones(hh) for hh in range(hb)]
        for r in range(tq // rq):
            for hh in range(hb):
                cols, keys, vals, pick = cols_of[hh], keys_of[hh], vals_of[hh], pick_of[hh]
                rows = pl.ds(r * rq, rq)
                n_c = (r + 1) * rq if diagonal else n_k
                q = jnp.concatenate([q_ref[rows, cols], pick], axis=1)
                s = lax.dot_general(q, keys[:n_c], (((1,), (1,)), ((), ())),
                                    preferred_element_type=F32)
                if diagonal:
                    tri = (lax.broadcasted_iota(jnp.int32, (rq, rq), 1)
                           <= lax.broadcasted_iota(jnp.int32, (rq, rq), 0))
                    s_last = jnp.where(tri, s[:, n_c - rq:], NEG_BIG)
                    s = s_last if r == 0 else jnp.concatenate(
                        [s[:, :n_c - rq], s_last], axis=1)
                m_prev = m_ref[hh, rows, :]
                m_next = jnp.maximum(m_prev, jnp.max(s, axis=1, keepdims=True))
                p = jnp.exp2(s - jnp.tile(m_next, (1, n_c // LANE)))
                alpha = jnp.exp2(m_prev - m_next)
                m_ref[hh, rows, :] = m_next
                acc_ref[hh, rows, :] = (jnp.tile(alpha, (1, 2)) * acc_ref[hh, rows, :]
                                        + jnp.dot(p.astype(BF16), vals[:n_c],
                                                  preferred_element_type=F32))

    @pl.when(j == 0)
    def _():
        m_ref[...] = jnp.full_like(m_ref, NEG_BIG)
        acc_ref[...] = jnp.zeros_like(acc_ref)
        online_update(km_ref, vm_ref, am_ref, False)

    @pl.when(jnp.logical_and(j >= 1, j - 1 < i))
    def _():
        online_update(kx_ref, vx_ref, ax_ref, False)

    @pl.when(j - 1 == i)
    def _():
        online_update(kx_ref, vx_ref, ax_ref, True)
        for hh in range(hb):
            acc = acc_ref[hh]
            o_ref[:, hh * dh:(hh + 1) * dh] = (acc[:, :dh] / acc[:, dh:]).astype(
                o_ref.dtype)


def _fox_attention(qkv, kaug, *, seq, tq, rq, hb):
    heads = FOX_HEADS
    dh = FOX_HEAD_DIM
    assert dh == LANE and seq % tq == 0 and tq % rq == 0 and rq % LANE == 0
    assert heads % hb == 0
    n_q = seq // tq
    n_g = heads // hb
    meta_blk = seq // META_BLOCK
    steps = [(i, j) for i in range(n_q) for j in range(i + 2)]
    i_tab = jnp.asarray([s[0] for s in steps], jnp.int32)
    j_tab = jnp.asarray([s[1] for s in steps], jnp.int32)
    wb = hb * dh

    def kv_block(t, j_tab):
        return jnp.maximum(j_tab[t] - 1, 0)

    grid_spec = pltpu.PrefetchScalarGridSpec(
        num_scalar_prefetch=2,
        grid=(n_g, len(steps)),
        in_specs=[
            pl.BlockSpec((tq, wb), lambda g, t, it, jt: (it[t], g)),
            pl.BlockSpec((META_BLOCK, wb), lambda g, t, it, jt: (meta_blk, n_g + g)),
            pl.BlockSpec((META_BLOCK, wb),
                         lambda g, t, it, jt: (meta_blk, 2 * n_g + g)),
            pl.BlockSpec((META_BLOCK, LANE), lambda g, t, it, jt: (meta_blk, 0)),
            pl.BlockSpec((tq, wb), lambda g, t, it, jt: (kv_block(t, jt), n_g + g)),
            pl.BlockSpec((tq, wb),
                         lambda g, t, it, jt: (kv_block(t, jt), 2 * n_g + g)),
            pl.BlockSpec((tq, LANE), lambda g, t, it, jt: (kv_block(t, jt), 0)),
        ],
        out_specs=pl.BlockSpec((tq, wb), lambda g, t, it, jt: (it[t], g)),
        scratch_shapes=[pltpu.VMEM((hb, tq, LANE), F32),
                        pltpu.VMEM((hb, tq, 2 * dh), F32)],
    )
    return pl.pallas_call(
        functools.partial(_fox_body, tq, rq, hb),
        grid_spec=grid_spec,
        out_shape=jax.ShapeDtypeStruct((seq, heads * dh), BF16),
        compiler_params=pltpu.CompilerParams(
            dimension_semantics=("parallel", "arbitrary"),
            vmem_limit_bytes=VMEM_LIMIT),
        name="fox_attention",
    )(i_tab, j_tab, qkv, qkv, qkv, kaug, qkv, qkv, kaug)


def _gla_body(n_chunks, qv_ref, kr_ref, ab_ref, w2_ref, b_ref, g_ref, o_ref, state_ref):
    t = pl.program_id(0)
    is_meta = t == 0

    @pl.when(is_meta)
    def _():
        state_ref[...] = jnp.zeros_like(state_ref)

    z = jnp.dot(ab_ref[...].astype(BF16), w2_ref[...].astype(BF16),
                preferred_element_type=F32) + b_ref[...]
    la = _log_sigmoid(z) / GLA_GATE_TAU
    n_rows = la.shape[0]
    row = lax.broadcasted_iota(jnp.int32, la.shape, 0)
    la = jnp.where(jnp.logical_and(is_meta, row >= N_META), 0.0, la)
    ri = lax.broadcasted_iota(jnp.int32, (n_rows, n_rows), 0)
    ci = lax.broadcasted_iota(jnp.int32, (n_rows, n_rows), 1)
    same_chunk = (ri // CHUNK) == (ci // CHUNK)
    lower = jnp.where(jnp.logical_and(same_chunk, ci <= ri), 1.0, 0.0).astype(BF16)
    cum = _tri_matmul(lower, la)
    totals = [cum[(c + 1) * CHUNK - 1:(c + 1) * CHUNK, :] for c in range(n_chunks)]
    total_rows = jnp.concatenate(
        [jnp.broadcast_to(tot, (CHUNK, tot.shape[1])) for tot in totals], axis=0)
    k_dec = (kr_ref[:, :GLA_KW] * jnp.exp(total_rows - cum)).astype(BF16)

    for c in range(n_chunks):
        rows = slice(c * CHUNK, (c + 1) * CHUNK)
        decay = jnp.exp(totals[c])
        for h in range(GLA_HEADS):
            ks = slice(h * GLA_DK, (h + 1) * GLA_DK)
            vs = slice(GLA_KW + h * GLA_DV, GLA_KW + (h + 1) * GLA_DV)
            kv = lax.dot_general(qv_ref[rows, vs], k_dec[rows, ks],
                                 (((0,), (0,)), ((), ())),
                                 preferred_element_type=F32)
            s_new = state_ref[h] * decay[:, ks] + kv
            state_ref[h] = s_new
            o = lax.dot_general(qv_ref[rows, ks], s_new.astype(BF16),
                                (((1,), (1,)), ((), ())),
                                preferred_element_type=F32)
            ms = jnp.mean(o * o, axis=-1, keepdims=True)
            r = kr_ref[rows, vs]
            o_ref[rows, h * GLA_DV:(h + 1) * GLA_DV] = (
                o * lax.rsqrt(ms + EPS) * g_ref[...] * (r * _sigmoid(r))
            ).astype(o_ref.dtype)


def _gla(qv, kr, ab, w2, b, g, *, seq, ab_off):
    rows = CHUNK
    assert rows % CHUNK == 0 and seq % rows == 0
    n_x = seq // rows
    w2 = jnp.zeros((LANE, GLA_KW), w2.dtype).at[ab_off:ab_off + GLA_GATE_RANK].set(w2)

    def row_block(t):
        return jnp.where(t == 0, n_x, t - 1)

    width = GLA_KW + GLA_VW
    return pl.pallas_call(
        functools.partial(_gla_body, rows // CHUNK),
        grid=(n_x + 1,),
        in_specs=[pl.BlockSpec((rows, width), lambda t: (row_block(t), 0)),
                  pl.BlockSpec((rows, width), lambda t: (row_block(t), 0)),
                  pl.BlockSpec((rows, LANE), lambda t: (row_block(t), 0)),
                  pl.BlockSpec((LANE, GLA_KW), lambda t: (0, 0)),
                  pl.BlockSpec((1, GLA_KW), lambda t: (0, 0)),
                  pl.BlockSpec((1, GLA_DV), lambda t: (0, 0))],
        out_specs=pl.BlockSpec((rows, GLA_VW), lambda t: (jnp.maximum(t - 1, 0), 0)),
        out_shape=jax.ShapeDtypeStruct((seq, GLA_VW), BF16),
        scratch_shapes=[pltpu.VMEM((GLA_HEADS, GLA_DV, GLA_DK), F32)],
        compiler_params=pltpu.CompilerParams(
            dimension_semantics=("arbitrary",), vmem_limit_bytes=VMEM_LIMIT),
        name="gla",
    )(qv, kr, ab, w2, b.reshape(1, GLA_KW), g.reshape(1, GLA_DV))


def kernel(x, meta_tokens, norm1_g, ffn1_w1, ffn1_w3, ffn1_w2, norm_mix_g, w_in,
           fox_f_bias, gla_alpha_w2, gla_alpha_b, gla_norm_g, w_gate, w_proj_fox,
           w_proj_gla, w_out, norm2_g, ffn2_w1, ffn2_w3, ffn2_w2, norm_final_g):
    batch, seq, d = x.shape
    assert batch == 1 and seq % 1024 == 0
    depth = norm1_g.shape[0]
    lp = seq + META_BLOCK
    assert lp % 8 == 0 and (lp // 8) % 16 == 0
    tm_all = lp // 8
    tm_x = seq // 8

    norm_tile = 2 * META_BLOCK
    meta_pad = jnp.concatenate(
        [meta_tokens.astype(x.dtype), jnp.zeros((norm_tile - N_META, d), x.dtype)], axis=0)
    x2 = x[0]

    o_fa = FOX_WIDTH * 3
    o_qb = o_fa + FOX_HEADS
    o_kb = o_qb + GLA_KW
    o_vb = o_kb + GLA_KW
    o_rb = o_vb + GLA_VW
    o_ab = o_rb + GLA_VW
    fox_scale = FOX_HEAD_DIM ** -0.5 * LOG2E
    gla_scale = GLA_DK ** -0.5

    assert depth == 1
    for li in range(depth):
        n1 = _rmsnorm_frames_meta(x2, meta_pad, norm1_g[li], lp=lp)
        h = _swiglu_ffn(n1, x2, ffn1_w1[li], ffn1_w3[li], ffn1_w2[li], m_rows=lp,
                        tm_up=2 * tm_all, tm_down=2 * tm_all,
                        res_tail=meta_pad[:META_BLOCK])

        n = _rmsnorm(h, norm_mix_g[li], lp, BF16)
        wt = jnp.transpose(w_in[li])

        tn_p = 512
        shift = o_qb % tn_p

        def tiles(start, width):
            assert start % tn_p in (0, shift) and width % tn_p == 0
            return [start - start % tn_p + r for r in range(0, width, tn_p)]

        def fox_epilogue(dot, j):
            return dot * jnp.where(j < FOX_WIDTH // tn_p, fox_scale, 1.0)

        qkv_a = _matmul_nt(n, wt, tiles(0, 3 * FOX_WIDTH), 0, fox_epilogue, m_rows=lp,
                           out_dtype=BF16, tm=tm_all, tn=tn_p, name="proj_fox")

        def gla_q_epilogue(dot, j):
            return dot * jnp.where(j < GLA_KW // tn_p, gla_scale, 1.0)

        qv_b = _matmul_nt(n, wt, tiles(o_qb, GLA_KW) + tiles(o_vb, GLA_VW), shift,
                          gla_q_epilogue, m_rows=lp, out_dtype=BF16,
                          tm=tm_all, tn=tn_p, name="proj_gla_qv")
        kr_b = _matmul_nt(n, wt, tiles(o_kb, GLA_KW) + tiles(o_rb, GLA_VW), shift,
                          lambda dot, j: dot, m_rows=lp, out_dtype=F32,
                          tm=tm_all, tn=tn_p, name="proj_gla_kr")

        kaug, ab = _small_proj(n, wt, fox_f_bias[li], seq=seq, f_col=o_fa, ab_col=o_ab)
        o_a = _fox_attention(qkv_a, kaug, seq=seq, tq=2048, rq=256, hb=2)
        o_b = _gla(qv_b, kr_b, ab, gla_alpha_w2[li], gla_alpha_b[li], gla_norm_g[li],
                   seq=seq, ab_off=o_ab % LANE)

        wg = w_gate[li]

        def mix_epilogue(accs, extras, j):
            ga, gb, pa, pb = accs
            return _sigmoid(ga) * pa + _sigmoid(gb) * pb

        tn_m = 256
        y = _matmul_fullk(
            [n, o_a, o_b],
            [(0, wg, 0), (0, wg, d // tn_m), (1, w_proj_fox[li], 0),
             (2, w_proj_gla[li], 0)],
            [], mix_epilogue, m_rows=seq, n_cols=d, out_dtype=BF16,
            tm=tm_x, tn=tn_m, name="mix", a_buffers=1)

        h = _matmul_fullk([y], [(0, w_out[li], 0)], [h],
                          lambda a, e, j: e[0] + a[0], m_rows=seq, n_cols=d,
                          out_dtype=F32, tm=tm_x, tn=512, name="out_proj")

        n2 = _rmsnorm(h, norm2_g[li], seq, BF16)
        h = _swiglu_ffn(n2, h, ffn2_w1[li], ffn2_w3[li], ffn2_w2[li], m_rows=seq,
                        tm_up=2 * tm_x, tm_down=2 * tm_x)

    out = _rmsnorm(h, norm_final_g, seq, F32)
    return out.reshape(batch, seq, d)
```

```python
import functools

import jax
import jax.numpy as jnp
from jax import lax
from jax.experimental import pallas as pl
from jax.experimental.pallas import tpu as pltpu

F32 = jnp.float32
BF16 = jnp.bfloat16

EPS = 1e-6
N_META = 16
CHUNK = 64
FOX_HEADS = 16
FOX_HEAD_DIM = 128
FOX_WIDTH = FOX_HEADS * FOX_HEAD_DIM
GLA_HEADS = 4
GLA_DK = 256
GLA_DV = 512
GLA_KW = GLA_HEADS * GLA_DK
GLA_VW = GLA_HEADS * GLA_DV
GLA_GATE_RANK = 16
GLA_GATE_TAU = 16.0

LANE = 128
META_BLOCK = LANE
NEG_BIG = -1e30
LOG2E = 1.4426950408889634
VMEM_LIMIT = 60 * 1024 * 1024


def _log_sigmoid(z):
    return jnp.minimum(z, 0.0) - jnp.log(1.0 + jnp.exp(-jnp.abs(z)))


def _sigmoid(z):
    return 1.0 / (1.0 + jnp.exp(-z))


def _tri_matmul(tri, x):
    hi = x.astype(BF16)
    rest = x - hi.astype(F32)
    mid = rest.astype(BF16)
    lo = (rest - mid.astype(F32)).astype(BF16)
    return (jnp.dot(tri, hi, preferred_element_type=F32)
            + jnp.dot(tri, mid, preferred_element_type=F32)
            + jnp.dot(tri, lo, preferred_element_type=F32))


def _rmsnorm_body(x_ref, g_ref, o_ref):
    x = x_ref[...]
    ms = jnp.mean(x * x, axis=-1, keepdims=True)
    o_ref[...] = (x * lax.rsqrt(ms + EPS) * g_ref[...]).astype(o_ref.dtype)


def _rmsnorm(x, g, rows, out_dtype):
    d = x.shape[1]
    tile = max(t for t in range(16, 513, 16) if rows % t == 0)
    return pl.pallas_call(
        _rmsnorm_body,
        grid=(rows // tile,),
        in_specs=[pl.BlockSpec((tile, d), lambda i: (i, 0)),
                  pl.BlockSpec((1, d), lambda i: (0, 0))],
        out_specs=pl.BlockSpec((tile, d), lambda i: (i, 0)),
        out_shape=jax.ShapeDtypeStruct((rows, d), out_dtype),
        compiler_params=pltpu.CompilerParams(
            dimension_semantics=("parallel",), vmem_limit_bytes=VMEM_LIMIT),
        name="rmsnorm",
    )(x, g.reshape(1, d))


def _rmsnorm_frames_meta_body(n_x, x_ref, mp_ref, g_ref, o_ref):
    src = jnp.where(pl.program_id(0) < n_x, x_ref[...], mp_ref[...])
    ms = jnp.mean(src * src, axis=-1, keepdims=True)
    o_ref[...] = (src * lax.rsqrt(ms + EPS) * g_ref[...]).astype(o_ref.dtype)


def _rmsnorm_frames_meta(x, meta_pad, g, *, lp):
    seq, d = x.shape
    tile = meta_pad.shape[0]
    assert seq % tile == 0 and seq < lp <= seq + tile
    n_x = seq // tile
    return pl.pallas_call(
        functools.partial(_rmsnorm_frames_meta_body, n_x),
        grid=(n_x + 1,),
        in_specs=[pl.BlockSpec((tile, d), lambda i: (jnp.minimum(i, n_x - 1), 0)),
                  pl.BlockSpec((tile, d), lambda i: (0, 0)),
                  pl.BlockSpec((1, d), lambda i: (0, 0))],
        out_specs=pl.BlockSpec((tile, d), lambda i: (i, 0)),
        out_shape=jax.ShapeDtypeStruct((lp, d), BF16),
        compiler_params=pltpu.CompilerParams(
            dimension_semantics=("parallel",), vmem_limit_bytes=VMEM_LIMIT),
        name="rmsnorm_in",
    )(x, meta_pad, g.reshape(1, d))


def _mm_fullk_body(pair_ai, n_a, n_extra, epilogue, *refs):
    n_p = len(pair_ai)
    a_refs = refs[:n_a]
    b_refs = refs[n_a:n_a + n_p]
    e_refs = refs[n_a + n_p:n_a + n_p + n_extra]
    o_ref = refs[n_a + n_p + n_extra]
    dots = [jnp.dot(a_refs[ai][...], b_refs[p][...].astype(BF16),
                    preferred_element_type=F32) for p, ai in enumerate(pair_ai)]
    extras = [r[...] for r in e_refs]
    o_ref[...] = epilogue(dots, extras, pl.program_id(1)).astype(o_ref.dtype)


def _matmul_fullk(a_list, pairs, extras, epilogue, *, m_rows, n_cols, out_dtype,
                  tm, tn, name, a_buffers=2):
    assert m_rows % tm == 0 and n_cols % tn == 0
    a_mode = {} if a_buffers == 2 else {"pipeline_mode": pl.Buffered(a_buffers)}
    in_specs = [pl.BlockSpec((tm, a.shape[1]), lambda i, j: (i, 0), **a_mode)
                for a in a_list]
    for ai, b, col_off in pairs:
        assert b.shape[0] == a_list[ai].shape[1]
        in_specs.append(pl.BlockSpec((b.shape[0], tn),
                                     lambda i, j, col_off=col_off: (0, j + col_off)))
    for _ in extras:
        in_specs.append(pl.BlockSpec((tm, tn), lambda i, j: (i, j)))
    body = functools.partial(_mm_fullk_body, tuple(ai for ai, _, _ in pairs),
                             len(a_list), len(extras), epilogue)
    return pl.pallas_call(
        body,
        grid=(m_rows // tm, n_cols // tn),
        in_specs=in_specs,
        out_specs=pl.BlockSpec((tm, tn), lambda i, j: (i, j)),
        out_shape=jax.ShapeDtypeStruct((m_rows, n_cols), out_dtype),
        compiler_params=pltpu.CompilerParams(
            dimension_semantics=("parallel", "arbitrary"),
            vmem_limit_bytes=VMEM_LIMIT),
        name=name,
    )(*a_list, *[b for _, b, _ in pairs], *extras)


def _mm_nt_body(shift, epilogue, tab_ref, a_ref, bm_ref, *rest):
    del tab_ref
    b = bm_ref[...]
    if shift:
        bn_ref, o_ref = rest
        b = jnp.concatenate([b[shift:], bn_ref[...]], axis=0)
    else:
        (o_ref,) = rest
    d = lax.dot_general(a_ref[...], b.astype(BF16), (((1,), (1,)), ((), ())),
                        preferred_element_type=F32)
    o_ref[...] = epilogue(d, pl.program_id(1)).astype(o_ref.dtype)


def _matmul_nt(a, bt, tile_starts, shift, epilogue, *, m_rows, out_dtype, tm, tn, name):
    assert m_rows % tm == 0 and all(s % tn == 0 for s in tile_starts)
    assert shift % 8 == 0 and (shift == 0 or tn % shift == 0)
    k_total = a.shape[1]
    assert bt.shape[1] == k_total
    tab = jnp.asarray([s // tn for s in tile_starts], jnp.int32)
    in_specs = [pl.BlockSpec((tm, k_total), lambda i, j, tab: (i, 0),
                             pipeline_mode=pl.Buffered(1)),
                pl.BlockSpec((tn, k_total), lambda i, j, tab: (tab[j], 0))]
    operands = [tab, a, bt]
    if shift:
        per = tn // shift
        in_specs.append(pl.BlockSpec((shift, k_total),
                                     lambda i, j, tab: ((tab[j] + 1) * per, 0)))
        operands.append(bt)
    grid_spec = pltpu.PrefetchScalarGridSpec(
        num_scalar_prefetch=1,
        grid=(m_rows // tm, len(tile_starts)),
        in_specs=in_specs,
        out_specs=pl.BlockSpec((tm, tn), lambda i, j, tab: (i, j)),
    )
    return pl.pallas_call(
        functools.partial(_mm_nt_body, shift, epilogue),
        grid_spec=grid_spec,
        out_shape=jax.ShapeDtypeStruct((m_rows, tn * len(tile_starts)), out_dtype),
        compiler_params=pltpu.CompilerParams(
            dimension_semantics=("parallel", "arbitrary"),
            vmem_limit_bytes=VMEM_LIMIT),
        name=name,
    )(*operands)


def _mm_kacc_body(tk, k_total, scale, has_tail, a_ref, b_ref, r_ref, *rest):
    if has_tail:
        tail_ref, o_ref = rest
    else:
        (o_ref,) = rest
    k = pl.program_id(2)
    last_row_block = pl.program_id(0) == pl.num_programs(0) - 1

    def residual():
        r = r_ref[...]
        if has_tail:
            tm, t_rows = r.shape[0], tail_ref.shape[0]
            tail = jnp.concatenate(
                [jnp.zeros((tm - t_rows, r.shape[1]), r.dtype), tail_ref[...]], axis=0)
            row = lax.broadcasted_iota(jnp.int32, r.shape, 0)
            in_tail = jnp.logical_and(last_row_block, row >= tm - t_rows)
            r = jnp.where(in_tail, tail, r)
        return r

    def partial_product():
        a = a_ref[...]
        b = b_ref[...]
        if k_total % tk:
            limit = k_total - k * tk
            a = jnp.where(lax.broadcasted_iota(jnp.int32, a.shape, 1) < limit, a,
                          jnp.zeros_like(a))
            b = jnp.where(lax.broadcasted_iota(jnp.int32, b.shape, 0) < limit, b,
                          jnp.zeros_like(b))
        return scale * jnp.dot(a, b.astype(BF16), preferred_element_type=F32)

    @pl.when(k == 0)
    def _():
        o_ref[...] = residual() + partial_product()

    @pl.when(k > 0)
    def _():
        o_ref[...] += partial_product()


def _matmul_kacc(a, b, res, scale, *, m_rows, tm, tn, tk, name, res_tail=None):
    k_total, n_cols = b.shape
    assert a.shape[1] == k_total and m_rows % tm == 0 and n_cols % tn == 0
    in_specs = [pl.BlockSpec((tm, tk), lambda i, j, k: (i, k)),
                pl.BlockSpec((tk, tn), lambda i, j, k: (k, j)),
                pl.BlockSpec((tm, tn), lambda i, j, k: (i, j))]
    operands = [a, b, res]
    if res_tail is not None:
        assert res.shape[0] + res_tail.shape[0] == m_rows
        in_specs.append(pl.BlockSpec((res_tail.shape[0], tn), lambda i, j, k: (0, j)))
        operands.append(res_tail)
    return pl.pallas_call(
        functools.partial(_mm_kacc_body, tk, k_total, scale, res_tail is not None),
        grid=(m_rows // tm, n_cols // tn, pl.cdiv(k_total, tk)),
        in_specs=in_specs,
        out_specs=pl.BlockSpec((tm, tn), lambda i, j, k: (i, j)),
        out_shape=jax.ShapeDtypeStruct((m_rows, n_cols), F32),
        compiler_params=pltpu.CompilerParams(
            dimension_semantics=("parallel", "parallel", "arbitrary"),
            vmem_limit_bytes=VMEM_LIMIT),
        name=name,
    )(*operands)


def _swiglu_ffn(n, h_res, w1, w3, w2, *, m_rows, tm_up, tm_down, res_tail=None):
    def up_epilogue(dots, extras, j):
        a1, a3 = dots
        return a1 * _sigmoid(a1) * a3

    u = _matmul_fullk([n], [(0, w1, 0), (0, w3, 0)], [], up_epilogue,
                      m_rows=m_rows, n_cols=w1.shape[1], out_dtype=BF16,
                      tm=tm_up, tn=256, name="ffn_up", a_buffers=1)
    return _matmul_kacc(u, w2, h_res, 0.5, m_rows=m_rows, tm=tm_down, tn=1024, tk=1024,
                        name="ffn_down", res_tail=res_tail)


def _small_proj_body(f_off, ab_off, n_ref, wf_ref, fb_ref, wab_ref, kaug_ref, ab_ref,
                     carry_ref):
    t = pl.program_id(0)
    is_meta = t == 0
    nt_dims = (((1,), (1,)), ((), ()))

    @pl.when(is_meta)
    def _():
        carry_ref[...] = jnp.zeros_like(carry_ref)

    n = n_ref[...]
    ft = lax.dot_general(n, wf_ref[...].astype(BF16), nt_dims,
                         preferred_element_type=F32)
    row = lax.broadcasted_iota(jnp.int32, ft.shape, 0)
    col = lax.broadcasted_iota(jnp.int32, ft.shape, 1)
    ft = jnp.where(jnp.logical_and(col >= f_off, col < f_off + FOX_HEADS), ft, 0.0)
    shifts = [(p * FOX_HEADS - f_off) % LANE for p in range(3)]
    ft = sum(pltpu.roll(ft, s, 1) if s else ft for s in shifts)
    logf = _log_sigmoid(ft + fb_ref[...])
    valid = jnp.logical_or(jnp.logical_not(is_meta), row < N_META)
    logf = jnp.where(valid, logf, 0.0)
    tsz = logf.shape[0]
    lower = jnp.where(lax.broadcasted_iota(jnp.int32, (tsz, tsz), 1)
                      <= lax.broadcasted_iota(jnp.int32, (tsz, tsz), 0),
                      1.0, 0.0).astype(BF16)
    c = carry_ref[0:1, :] + _tri_matmul(lower, logf)
    carry_ref[...] = carry_ref[...] + jnp.sum(logf, axis=0, keepdims=True)

    bias = -LOG2E * c
    hi = bias.astype(BF16).astype(F32)
    mid = (bias - hi).astype(BF16).astype(F32)
    lo = (bias - hi - mid).astype(BF16).astype(F32)
    pieces = jnp.where(col < FOX_HEADS, hi,
                       jnp.where(col < 2 * FOX_HEADS, mid,
                                 jnp.where(col < 3 * FOX_HEADS, lo, 0.0)))
    removed = jnp.where(col < FOX_HEADS, NEG_BIG, 0.0)
    kaug_ref[...] = jnp.where(valid, pieces, removed).astype(BF16)
    wrow = lax.broadcasted_iota(jnp.int32, wab_ref.shape, 0)
    wab = jnp.where(jnp.logical_and(wrow >= ab_off, wrow < ab_off + GLA_GATE_RANK),
                    wab_ref[...], 0.0)
    ab_ref[...] = lax.dot_general(n, wab.astype(BF16), nt_dims,
                                  preferred_element_type=F32)


def _small_proj(n, wt, f_bias, *, seq, f_col, ab_col):
    lp, d = n.shape
    heads = f_bias.shape[0]
    assert heads == FOX_HEADS and 3 * heads <= LANE
    assert f_col % LANE + heads <= LANE and ab_col % LANE + GLA_GATE_RANK <= LANE
    tsz = 4 * META_BLOCK
    assert seq % tsz == 0
    n_x = seq // tsz
    fb3 = jnp.concatenate([f_bias, f_bias, f_bias,
                           jnp.zeros((LANE - 3 * heads,), f_bias.dtype)])

    def row_block(t):
        return jnp.where(t == 0, n_x, t - 1)

    return pl.pallas_call(
        functools.partial(_small_proj_body, f_col % LANE, ab_col % LANE),
        grid=(n_x + 1,),
        in_specs=[pl.BlockSpec((tsz, d), lambda t: (row_block(t), 0)),
                  pl.BlockSpec((LANE, d), lambda t: (f_col // LANE, 0)),
                  pl.BlockSpec((1, LANE), lambda t: (0, 0)),
                  pl.BlockSpec((LANE, d), lambda t: (ab_col // LANE, 0))],
        out_specs=[pl.BlockSpec((tsz, LANE), lambda t: (row_block(t), 0)),
                   pl.BlockSpec((tsz, LANE), lambda t: (row_block(t), 0))],
        out_shape=[jax.ShapeDtypeStruct((lp, LANE), BF16),
                   jax.ShapeDtypeStruct((lp, LANE), F32)],
        scratch_shapes=[pltpu.VMEM((8, LANE), F32)],
        compiler_params=pltpu.CompilerParams(
            dimension_semantics=("arbitrary",), vmem_limit_bytes=VMEM_LIMIT),
        name="small_proj",
    )(n, wt, fb3.reshape(1, LANE), wt)


def _fox_body(tq, rq, hb, i_tab, j_tab, q_ref, km_ref, vm_ref, am_ref, kx_ref, vx_ref,
              ax_ref, o_ref, m_ref, acc_ref):
    g = pl.program_id(0)
    t = pl.program_id(1)
    i = i_tab[t]
    j = j_tab[t]
    dh = LANE
    lane = lax.broadcasted_iota(jnp.int32, (rq, LANE), 1)

    def q_ones(hh):
        h = g * hb + hh
        pick = jnp.logical_or(lane == h, jnp.logical_or(lane == h + FOX_HEADS,
                                                        lane == h + 2 * FOX_HEADS))
        return jnp.where(pick, 1.0, 0.0).astype(BF16)

    def online_update(k_ref, v_ref, a_ref, diagonal):
        n_k = k_ref.shape[0]
        ones = jnp.ones((n_k, LANE), BF16)
        aug = a_ref[...]
        cols_of = [slice(hh * dh, (hh + 1) * dh) for hh in range(hb)]
        keys_of = [jnp.concatenate([k_ref[:, c], aug], axis=1) for c in cols_of]
        vals_of = [jnp.concatenate([v_ref[:, c], ones], axis=1) for c in cols_of]
        pick_of = [q_ones(hh) for hh in range(hb)]
        for r in range(tq // rq):
            for hh in range(hb):
                cols, keys, vals, pick = cols_of[hh], keys_of[hh], vals_of[hh], pick_of[hh]
                rows = pl.ds(r * rq, rq)
                n_c = (r + 1) * rq if diagonal else n_k
                q = jnp.concatenate([q_ref[rows, cols], pick], axis=1)
                s = lax.dot_general(q, keys[:n_c], (((1,), (1,)), ((), ())),
                                    preferred_element_type=F32)
                if diagonal:
                    tri = (lax.broadcasted_iota(jnp.int32, (rq, rq), 1)
                           <= lax.broadcasted_iota(jnp.int32, (rq, rq), 0))
                    s_last = jnp.where(tri, s[:, n_c - rq:], NEG_BIG)
                    s = s_last if r == 0 else jnp.concatenate(
                        [s[:, :n_c - rq], s_last], axis=1)
                m_prev = m_ref[hh, rows, :]
                m_next = jnp.maximum(m_prev, jnp.max(s, axis=1, keepdims=True))
                p = jnp.exp2(s - jnp.tile(m_next, (1, n_c // LANE)))
                alpha = jnp.exp2(m_prev - m_next)
                m_ref[hh, rows, :] = m_next
                acc_ref[hh, rows, :] = (jnp.tile(alpha, (1, 2)) * acc_ref[hh, rows, :]
                                        + jnp.dot(p.astype(BF16), vals[:n_c],
                                                  preferred_element_type=F32))

    @pl.when(j == 0)
    def _():
        m_ref[...] = jnp.full_like(m_ref, NEG_BIG)
        acc_ref[...] = jnp.zeros_like(acc_ref)
        online_update(km_ref, vm_ref, am_ref, False)

    @pl.when(jnp.logical_and(j >= 1, j - 1 < i))
    def _():
        online_update(kx_ref, vx_ref, ax_ref, False)

    @pl.when(j - 1 == i)
    def _():
        online_update(kx_ref, vx_ref, ax_ref, True)
        for hh in range(hb):
            acc = acc_ref[hh]
            o_ref[:, hh * dh:(hh + 1) * dh] = (acc[:, :dh] / acc[:, dh:]).astype(
                o_ref.dtype)


def _fox_attention(qkv, kaug, *, seq, tq, rq, hb):
    heads = FOX_HEADS
    dh = FOX_HEAD_DIM
    assert dh == LANE and seq % tq == 0 and tq % rq == 0 and rq % LANE == 0
    assert heads % hb == 0
    n_q = seq // tq
    n_g = heads // hb
    meta_blk = seq // META_BLOCK
    steps = [(i, j) for i in range(n_q) for j in range(i + 2)]
    i_tab = jnp.asarray([s[0] for s in steps], jnp.int32)
    j_tab = jnp.asarray([s[1] for s in steps], jnp.int32)
    wb = hb * dh

    def kv_block(t, j_tab):
        return jnp.maximum(j_tab[t] - 1, 0)

    grid_spec = pltpu.PrefetchScalarGridSpec(
        num_scalar_prefetch=2,
        grid=(n_g, len(steps)),
        in_specs=[
            pl.BlockSpec((tq, wb), lambda g, t, it, jt: (it[t], g)),
            pl.BlockSpec((META_BLOCK, wb), lambda g, t, it, jt: (meta_blk, n_g + g)),
            pl.BlockSpec((META_BLOCK, wb),
                         lambda g, t, it, jt: (meta_blk, 2 * n_g + g)),
            pl.BlockSpec((META_BLOCK, LANE), lambda g, t, it, jt: (meta_blk, 0)),
            pl.BlockSpec((tq, wb), lambda g, t, it, jt: (kv_block(t, jt), n_g + g)),
            pl.BlockSpec((tq, wb),
                         lambda g, t, it, jt: (kv_block(t, jt), 2 * n_g + g)),
            pl.BlockSpec((tq, LANE), lambda g, t, it, jt: (kv_block(t, jt), 0)),
        ],
        out_specs=pl.BlockSpec((tq, wb), lambda g, t, it, jt: (it[t], g)),
        scratch_shapes=[pltpu.VMEM((hb, tq, LANE), F32),
                        pltpu.VMEM((hb, tq, 2 * dh), F32)],
    )
    return pl.pallas_call(
        functools.partial(_fox_body, tq, rq, hb),
        grid_spec=grid_spec,
        out_shape=jax.ShapeDtypeStruct((seq, heads * dh), BF16),
        compiler_params=pltpu.CompilerParams(
            dimension_semantics=("parallel", "arbitrary"),
            vmem_limit_bytes=VMEM_LIMIT),
        name="fox_attention",
    )(i_tab, j_tab, qkv, qkv, qkv, kaug, qkv, qkv, kaug)


def _gla_body(n_chunks, qv_ref, kr_ref, ab_ref, w2_ref, b_ref, g_ref, o_ref, state_ref):
    t = pl.program_id(0)
    is_meta = t == 0

    @pl.when(is_meta)
    def _():
        state_ref[...] = jnp.zeros_like(state_ref)

    z = jnp.dot(ab_ref[...].astype(BF16), w2_ref[...].astype(BF16),
                preferred_element_type=F32) + b_ref[...]
    la = _log_sigmoid(z) / GLA_GATE_TAU
    n_rows = la.shape[0]
    row = lax.broadcasted_iota(jnp.int32, la.shape, 0)
    la = jnp.where(jnp.logical_and(is_meta, row >= N_META), 0.0, la)
    ri = lax.broadcasted_iota(jnp.int32, (n_rows, n_rows), 0)
    ci = lax.broadcasted_iota(jnp.int32, (n_rows, n_rows), 1)
    same_chunk = (ri // CHUNK) == (ci // CHUNK)
    lower = jnp.where(jnp.logical_and(same_chunk, ci <= ri), 1.0, 0.0).astype(BF16)
    cum = _tri_matmul(lower, la)
    totals = [cum[(c + 1) * CHUNK - 1:(c + 1) * CHUNK, :] for c in range(n_chunks)]
    total_rows = jnp.concatenate(
        [jnp.broadcast_to(tot, (CHUNK, tot.shape[1])) for tot in totals], axis=0)
    k_dec = (kr_ref[:, :GLA_KW] * jnp.exp(total_rows - cum)).astype(BF16)

    for c in range(n_chunks):
        rows = slice(c * CHUNK, (c + 1) * CHUNK)
        decay = jnp.exp(totals[c])
        for h in range(GLA_HEADS):
            ks = slice(h * GLA_DK, (h + 1) * GLA_DK)
            vs = slice(GLA_KW + h * GLA_DV, GLA_KW + (h + 1) * GLA_DV)
            kv = lax.dot_general(qv_ref[rows, vs], k_dec[rows, ks],
                                 (((0,), (0,)), ((), ())),
                                 preferred_element_type=F32)
            s_new = state_ref[h] * decay[:, ks] + kv
            state_ref[h] = s_new
            o = lax.dot_general(qv_ref[rows, ks], s_new.astype(BF16),
                                (((1,), (1,)), ((), ())),
                                preferred_element_type=F32)
            ms = jnp.mean(o * o, axis=-1, keepdims=True)
            r = kr_ref[rows, vs]
            o_ref[rows, h * GLA_DV:(h + 1) * GLA_DV] = (
                o * lax.rsqrt(ms + EPS) * g_ref[...] * (r * _sigmoid(r))
            ).astype(o_ref.dtype)


def _gla(qv, kr, ab, w2, b, g, *, seq, ab_off):
    rows = CHUNK
    assert rows % CHUNK == 0 and seq % rows == 0
    n_x = seq // rows
    w2 = jnp.zeros((LANE, GLA_KW), w2.dtype).at[ab_off:ab_off + GLA_GATE_RANK].set(w2)

    def row_block(t):
        return jnp.where(t == 0, n_x, t - 1)

    width = GLA_KW + GLA_VW
    return pl.pallas_call(
        functools.partial(_gla_body, rows // CHUNK),
        grid=(n_x + 1,),
        in_specs=[pl.BlockSpec((rows, width), lambda t: (row_block(t), 0)),
                  pl.BlockSpec((rows, width), lambda t: (row_block(t), 0)),
                  pl.BlockSpec((rows, LANE), lambda t: (row_block(t), 0)),
                  pl.BlockSpec((LANE, GLA_KW), lambda t: (0, 0)),
                  pl.BlockSpec((1, GLA_KW), lambda t: (0, 0)),
                  pl.BlockSpec((1, GLA_DV), lambda t: (0, 0))],
        out_specs=pl.BlockSpec((rows, GLA_VW), lambda t: (jnp.maximum(t - 1, 0), 0)),
        out_shape=jax.ShapeDtypeStruct((seq, GLA_VW), BF16),
        scratch_shapes=[pltpu.VMEM((GLA_HEADS, GLA_DV, GLA_DK), F32)],
        compiler_params=pltpu.CompilerParams(
            dimension_semantics=("arbitrary",), vmem_limit_bytes=VMEM_LIMIT),
        name="gla",
    )(qv, kr, ab, w2, b.reshape(1, GLA_KW), g.reshape(1, GLA_DV))


def kernel(x, meta_tokens, norm1_g, ffn1_w1, ffn1_w3, ffn1_w2, norm_mix_g, w_in,
           fox_f_bias, gla_alpha_w2, gla_alpha_b, gla_norm_g, w_gate, w_proj_fox,
           w_proj_gla, w_out, norm2_g, ffn2_w1, ffn2_w3, ffn2_w2, norm_final_g):
    batch, seq, d = x.shape
    assert batch == 1 and seq % 1024 == 0
    depth = norm1_g.shape[0]
    lp = seq + META_BLOCK
    assert lp % 8 == 0 and (lp // 8) % 16 == 0
    tm_all = lp // 8
    tm_x = seq // 8

    norm_tile = 2 * META_BLOCK
    meta_pad = jnp.concatenate(
        [meta_tokens.astype(x.dtype), jnp.zeros((norm_tile - N_META, d), x.dtype)], axis=0)
    x2 = x[0]

    o_fa = FOX_WIDTH * 3
    o_qb = o_fa + FOX_HEADS
    o_kb = o_qb + GLA_KW
    o_vb = o_kb + GLA_KW
    o_rb = o_vb + GLA_VW
    o_ab = o_rb + GLA_VW
    fox_scale = FOX_HEAD_DIM ** -0.5 * LOG2E
    gla_scale = GLA_DK ** -0.5

    assert depth == 1
    for li in range(depth):
        n1 = _rmsnorm_frames_meta(x2, meta_pad, norm1_g[li], lp=lp)
        h = _swiglu_ffn(n1, x2, ffn1_w1[li], ffn1_w3[li], ffn1_w2[li], m_rows=lp,
                        tm_up=2 * tm_all, tm_down=2 * tm_all,
                        res_tail=meta_pad[:META_BLOCK])

        n = _rmsnorm(h, norm_mix_g[li], lp, BF16)
        wt = jnp.transpose(w_in[li])

        tn_p = 512
        shift = o_qb % tn_p

        def tiles(start, width):
            assert start % tn_p in (0, shift) and width % tn_p == 0
            return [start - start % tn_p + r for r in range(0, width, tn_p)]

        def fox_epilogue(dot, j):
            return dot * jnp.where(j < FOX_WIDTH // tn_p, fox_scale, 1.0)

        qkv_a = _matmul_nt(n, wt, tiles(0, 3 * FOX_WIDTH), 0, fox_epilogue, m_rows=lp,
                           out_dtype=BF16, tm=2 * tm_all, tn=tn_p, name="proj_fox")

        def gla_q_epilogue(dot, j):
            return dot * jnp.where(j < GLA_KW // tn_p, gla_scale, 1.0)

        qv_b = _matmul_nt(n, wt, tiles(o_qb, GLA_KW) + tiles(o_vb, GLA_VW), shift,
                          gla_q_epilogue, m_rows=lp, out_dtype=BF16,
                          tm=2 * tm_all, tn=tn_p, name="proj_gla_qv")
        kr_b = _matmul_nt(n, wt, tiles(o_kb, GLA_KW) + tiles(o_rb, GLA_VW), shift,
                          lambda dot, j: dot, m_rows=lp, out_dtype=F32,
                          tm=2 * tm_all, tn=tn_p, name="proj_gla_kr")

        kaug, ab = _small_proj(n, wt, fox_f_bias[li], seq=seq, f_col=o_fa, ab_col=o_ab)
        o_a = _fox_attention(qkv_a, kaug, seq=seq, tq=2048, rq=256, hb=2)
        o_b = _gla(qv_b, kr_b, ab, gla_alpha_w2[li], gla_alpha_b[li], gla_norm_g[li],
                   seq=seq, ab_off=o_ab % LANE)

        wg = w_gate[li]

        def mix_epilogue(accs, extras, j):
            ga, gb, pa, pb = accs
            return _sigmoid(ga) * pa + _sigmoid(gb) * pb

        tn_m = 256
        y = _matmul_fullk(
            [n, o_a, o_b],
            [(0, wg, 0), (0, wg, d // tn_m), (1, w_proj_fox[li], 0),
             (2, w_proj_gla[li], 0)],
            [], mix_epilogue, m_rows=seq, n_cols=d, out_dtype=BF16,
            tm=tm_x, tn=tn_m, name="mix", a_buffers=1)

        h = _matmul_fullk([y], [(0, w_out[li], 0)], [h],
                          lambda a, e, j: e[0] + a[0], m_rows=seq, n_cols=d,
                          out_dtype=F32, tm=2 * tm_x, tn=512, name="out_proj", a_buffers=1)

        n2 = _rmsnorm(h, norm2_g[li], seq, BF16)
        h = _swiglu_ffn(n2, h, ffn2_w1[li], ffn2_w3[li], ffn2_w2[li], m_rows=seq,
                        tm_up=2 * tm_x, tm_down=2 * tm_x)

    out = _rmsnorm(h, norm_final_g, seq, F32)
    return out.reshape(batch, seq, d)
```

```python
import functools

import jax
import jax.numpy as jnp
from jax import lax
from jax.experimental import pallas as pl
from jax.experimental.pallas import tpu as pltpu

F32 = jnp.float32
BF16 = jnp.bfloat16

EPS = 1e-6
N_META = 16
CHUNK = 64
FOX_HEADS = 16
FOX_HEAD_DIM = 128
FOX_WIDTH = FOX_HEADS * FOX_HEAD_DIM
GLA_HEADS = 4
GLA_DK = 256
GLA_DV = 512
GLA_KW = GLA_HEADS * GLA_DK
GLA_VW = GLA_HEADS * GLA_DV
GLA_GATE_RANK = 16
GLA_GATE_TAU = 16.0

LANE = 128
META_BLOCK = LANE
NEG_BIG = -1e30
LOG2E = 1.4426950408889634
VMEM_LIMIT = 60 * 1024 * 1024


def _log_sigmoid(z):
    return jnp.minimum(z, 0.0) - jnp.log(1.0 + jnp.exp(-jnp.abs(z)))


def _sigmoid(z):
    return 1.0 / (1.0 + jnp.exp(-z))


def _tri_matmul(tri, x):
    hi = x.astype(BF16)
    rest = x - hi.astype(F32)
    mid = rest.astype(BF16)
    lo = (rest - mid.astype(F32)).astype(BF16)
    return (jnp.dot(tri, hi, preferred_element_type=F32)
            + jnp.dot(tri, mid, preferred_element_type=F32)
            + jnp.dot(tri, lo, preferred_element_type=F32))


def _rmsnorm_body(x_ref, g_ref, o_ref):
    x = x_ref[...]
    ms = jnp.mean(x * x, axis=-1, keepdims=True)
    o_ref[...] = (x * lax.rsqrt(ms + EPS) * g_ref[...]).astype(o_ref.dtype)


def _rmsnorm(x, g, rows, out_dtype):
    d = x.shape[1]
    tile = max(t for t in range(16, 513, 16) if rows % t == 0)
    return pl.pallas_call(
        _rmsnorm_body,
        grid=(rows // tile,),
        in_specs=[pl.BlockSpec((tile, d), lambda i: (i, 0)),
                  pl.BlockSpec((1, d), lambda i: (0, 0))],
        out_specs=pl.BlockSpec((tile, d), lambda i: (i, 0)),
        out_shape=jax.ShapeDtypeStruct((rows, d), out_dtype),
        compiler_params=pltpu.CompilerParams(
            dimension_semantics=("parallel",), vmem_limit_bytes=VMEM_LIMIT),
        name="rmsnorm",
    )(x, g.reshape(1, d))


def _rmsnorm_frames_meta_body(n_x, x_ref, mp_ref, g_ref, o_ref):
    src = jnp.where(pl.program_id(0) < n_x, x_ref[...], mp_ref[...])
    ms = jnp.mean(src * src, axis=-1, keepdims=True)
    o_ref[...] = (src * lax.rsqrt(ms + EPS) * g_ref[...]).astype(o_ref.dtype)


def _rmsnorm_frames_meta(x, meta_pad, g, *, lp):
    seq, d = x.shape
    tile = meta_pad.shape[0]
    assert seq % tile == 0 and seq < lp <= seq + tile
    n_x = seq // tile
    return pl.pallas_call(
        functools.partial(_rmsnorm_frames_meta_body, n_x),
        grid=(n_x + 1,),
        in_specs=[pl.BlockSpec((tile, d), lambda i: (jnp.minimum(i, n_x - 1), 0)),
                  pl.BlockSpec((tile, d), lambda i: (0, 0)),
                  pl.BlockSpec((1, d), lambda i: (0, 0))],
        out_specs=pl.BlockSpec((tile, d), lambda i: (i, 0)),
        out_shape=jax.ShapeDtypeStruct((lp, d), BF16),
        compiler_params=pltpu.CompilerParams(
            dimension_semantics=("parallel",), vmem_limit_bytes=VMEM_LIMIT),
        name="rmsnorm_in",
    )(x, meta_pad, g.reshape(1, d))


def _mm_fullk_body(pair_ai, n_a, n_extra, epilogue, *refs):
    n_p = len(pair_ai)
    a_refs = refs[:n_a]
    b_refs = refs[n_a:n_a + n_p]
    e_refs = refs[n_a + n_p:n_a + n_p + n_extra]
    o_ref = refs[n_a + n_p + n_extra]
    dots = [jnp.dot(a_refs[ai][...], b_refs[p][...].astype(BF16),
                    preferred_element_type=F32) for p, ai in enumerate(pair_ai)]
    extras = [r[...] for r in e_refs]
    o_ref[...] = epilogue(dots, extras, pl.program_id(1)).astype(o_ref.dtype)


def _matmul_fullk(a_list, pairs, extras, epilogue, *, m_rows, n_cols, out_dtype,
                  tm, tn, name, a_buffers=2):
    assert m_rows % tm == 0 and n_cols % tn == 0
    if isinstance(a_buffers, int):
        a_buffers = [a_buffers] * len(a_list)
    in_specs = [pl.BlockSpec((tm, a.shape[1]), lambda i, j: (i, 0),
                             **({} if nb == 2 else {"pipeline_mode": pl.Buffered(nb)}))
                for a, nb in zip(a_list, a_buffers)]
    for ai, b, col_off in pairs:
        assert b.shape[0] == a_list[ai].shape[1]
        in_specs.append(pl.BlockSpec((b.shape[0], tn),
                                     lambda i, j, col_off=col_off: (0, j + col_off)))
    for _ in extras:
        in_specs.append(pl.BlockSpec((tm, tn), lambda i, j: (i, j)))
    body = functools.partial(_mm_fullk_body, tuple(ai for ai, _, _ in pairs),
                             len(a_list), len(extras), epilogue)
    return pl.pallas_call(
        body,
        grid=(m_rows // tm, n_cols // tn),
        in_specs=in_specs,
        out_specs=pl.BlockSpec((tm, tn), lambda i, j: (i, j)),
        out_shape=jax.ShapeDtypeStruct((m_rows, n_cols), out_dtype),
        compiler_params=pltpu.CompilerParams(
            dimension_semantics=("parallel", "arbitrary"),
            vmem_limit_bytes=VMEM_LIMIT),
        name=name,
    )(*a_list, *[b for _, b, _ in pairs], *extras)


def _mm_nt_body(shift, epilogue, tab_ref, a_ref, bm_ref, *rest):
    del tab_ref
    b = bm_ref[...]
    if shift:
        bn_ref, o_ref = rest
        b = jnp.concatenate([b[shift:], bn_ref[...]], axis=0)
    else:
        (o_ref,) = rest
    d = lax.dot_general(a_ref[...], b.astype(BF16), (((1,), (1,)), ((), ())),
                        preferred_element_type=F32)
    o_ref[...] = epilogue(d, pl.program_id(1)).astype(o_ref.dtype)


def _matmul_nt(a, bt, tile_starts, shift, epilogue, *, m_rows, out_dtype, tm, tn, name):
    assert m_rows % tm == 0 and all(s % tn == 0 for s in tile_starts)
    assert shift % 8 == 0 and (shift == 0 or tn % shift == 0)
    k_total = a.shape[1]
    assert bt.shape[1] == k_total
    tab = jnp.asarray([s // tn for s in tile_starts], jnp.int32)
    in_specs = [pl.BlockSpec((tm, k_total), lambda i, j, tab: (i, 0),
                             pipeline_mode=pl.Buffered(1)),
                pl.BlockSpec((tn, k_total), lambda i, j, tab: (tab[j], 0))]
    operands = [tab, a, bt]
    if shift:
        per = tn // shift
        in_specs.append(pl.BlockSpec((shift, k_total),
                                     lambda i, j, tab: ((tab[j] + 1) * per, 0)))
        operands.append(bt)
    grid_spec = pltpu.PrefetchScalarGridSpec(
        num_scalar_prefetch=1,
        grid=(m_rows // tm, len(tile_starts)),
        in_specs=in_specs,
        out_specs=pl.BlockSpec((tm, tn), lambda i, j, tab: (i, j)),
    )
    return pl.pallas_call(
        functools.partial(_mm_nt_body, shift, epilogue),
        grid_spec=grid_spec,
        out_shape=jax.ShapeDtypeStruct((m_rows, tn * len(tile_starts)), out_dtype),
        compiler_params=pltpu.CompilerParams(
            dimension_semantics=("parallel", "arbitrary"),
            vmem_limit_bytes=VMEM_LIMIT),
        name=name,
    )(*operands)


def _mm_kacc_body(tk, k_total, scale, has_tail, a_ref, b_ref, r_ref, *rest):
    if has_tail:
        tail_ref, o_ref = rest
    else:
        (o_ref,) = rest
    k = pl.program_id(2)
    last_row_block = pl.program_id(0) == pl.num_programs(0) - 1

    def residual():
        r = r_ref[...]
        if has_tail:
            tm, t_rows = r.shape[0], tail_ref.shape[0]
            tail = jnp.concatenate(
                [jnp.zeros((tm - t_rows, r.shape[1]), r.dtype), tail_ref[...]], axis=0)
            row = lax.broadcasted_iota(jnp.int32, r.shape, 0)
            in_tail = jnp.logical_and(last_row_block, row >= tm - t_rows)
            r = jnp.where(in_tail, tail, r)
        return r

    def partial_product():
        a = a_ref[...]
        b = b_ref[...]
        if k_total % tk:
            limit = k_total - k * tk
            a = jnp.where(lax.broadcasted_iota(jnp.int32, a.shape, 1) < limit, a,
                          jnp.zeros_like(a))
            b = jnp.where(lax.broadcasted_iota(jnp.int32, b.shape, 0) < limit, b,
                          jnp.zeros_like(b))
        return scale * jnp.dot(a, b.astype(BF16), preferred_element_type=F32)

    @pl.when(k == 0)
    def _():
        o_ref[...] = residual() + partial_product()

    @pl.when(k > 0)
    def _():
        o_ref[...] += partial_product()


def _matmul_kacc(a, b, res, scale, *, m_rows, tm, tn, tk, name, res_tail=None):
    k_total, n_cols = b.shape
    assert a.shape[1] == k_total and m_rows % tm == 0 and n_cols % tn == 0
    in_specs = [pl.BlockSpec((tm, tk), lambda i, j, k: (i, k)),
                pl.BlockSpec((tk, tn), lambda i, j, k: (k, j)),
                pl.BlockSpec((tm, tn), lambda i, j, k: (i, j))]
    operands = [a, b, res]
    if res_tail is not None:
        assert res.shape[0] + res_tail.shape[0] == m_rows
        in_specs.append(pl.BlockSpec((res_tail.shape[0], tn), lambda i, j, k: (0, j)))
        operands.append(res_tail)
    return pl.pallas_call(
        functools.partial(_mm_kacc_body, tk, k_total, scale, res_tail is not None),
        grid=(m_rows // tm, n_cols // tn, pl.cdiv(k_total, tk)),
        in_specs=in_specs,
        out_specs=pl.BlockSpec((tm, tn), lambda i, j, k: (i, j)),
        out_shape=jax.ShapeDtypeStruct((m_rows, n_cols), F32),
        compiler_params=pltpu.CompilerParams(
            dimension_semantics=("parallel", "parallel", "arbitrary"),
            vmem_limit_bytes=VMEM_LIMIT),
        name=name,
    )(*operands)


def _swiglu_ffn(n, h_res, w1, w3, w2, *, m_rows, tm_up, tm_down, res_tail=None):
    def up_epilogue(dots, extras, j):
        a1, a3 = dots
        return a1 * _sigmoid(a1) * a3

    u = _matmul_fullk([n], [(0, w1, 0), (0, w3, 0)], [], up_epilogue,
                      m_rows=m_rows, n_cols=w1.shape[1], out_dtype=BF16,
                      tm=tm_up, tn=256, name="ffn_up", a_buffers=1)
    return _matmul_kacc(u, w2, h_res, 0.5, m_rows=m_rows, tm=tm_down, tn=1024, tk=1024,
                        name="ffn_down", res_tail=res_tail)


def _small_proj_body(f_off, ab_off, n_ref, wf_ref, fb_ref, wab_ref, kaug_ref, ab_ref,
                     carry_ref):
    t = pl.program_id(0)
    is_meta = t == 0
    nt_dims = (((1,), (1,)), ((), ()))

    @pl.when(is_meta)
    def _():
        carry_ref[...] = jnp.zeros_like(carry_ref)

    n = n_ref[...]
    wrow = lax.broadcasted_iota(jnp.int32, wab_ref.shape, 0)
    wab = jnp.where(jnp.logical_and(wrow >= ab_off, wrow < ab_off + GLA_GATE_RANK),
                    wab_ref[...], 0.0)
    both = lax.dot_general(n, jnp.concatenate([wf_ref[...], wab], axis=0).astype(BF16),
                           nt_dims, preferred_element_type=F32)
    ft = both[:, :LANE]
    ab_ref[...] = both[:, LANE:]
    row = lax.broadcasted_iota(jnp.int32, ft.shape, 0)
    col = lax.broadcasted_iota(jnp.int32, ft.shape, 1)
    ft = jnp.where(jnp.logical_and(col >= f_off, col < f_off + FOX_HEADS), ft, 0.0)
    shifts = [(p * FOX_HEADS - f_off) % LANE for p in range(3)]
    ft = sum(pltpu.roll(ft, s, 1) if s else ft for s in shifts)
    logf = _log_sigmoid(ft + fb_ref[...])
    valid = jnp.logical_or(jnp.logical_not(is_meta), row < N_META)
    logf = jnp.where(valid, logf, 0.0)
    tsz = logf.shape[0]
    lower = jnp.where(lax.broadcasted_iota(jnp.int32, (tsz, tsz), 1)
                      <= lax.broadcasted_iota(jnp.int32, (tsz, tsz), 0),
                      1.0, 0.0).astype(BF16)
    c = carry_ref[0:1, :] + _tri_matmul(lower, logf)
    carry_ref[...] = carry_ref[...] + jnp.sum(logf, axis=0, keepdims=True)

    bias = -LOG2E * c
    hi = bias.astype(BF16).astype(F32)
    mid = (bias - hi).astype(BF16).astype(F32)
    lo = (bias - hi - mid).astype(BF16).astype(F32)
    pieces = jnp.where(col < FOX_HEADS, hi,
                       jnp.where(col < 2 * FOX_HEADS, mid,
                                 jnp.where(col < 3 * FOX_HEADS, lo, 0.0)))
    removed = jnp.where(col < FOX_HEADS, NEG_BIG, 0.0)
    kaug_ref[...] = jnp.where(valid, pieces, removed).astype(BF16)


def _small_proj(n, wt, f_bias, *, seq, f_col, ab_col):
    lp, d = n.shape
    heads = f_bias.shape[0]
    assert heads == FOX_HEADS and 3 * heads <= LANE
    assert f_col % LANE + heads <= LANE and ab_col % LANE + GLA_GATE_RANK <= LANE
    tsz = 4 * META_BLOCK
    assert seq % tsz == 0
    n_x = seq // tsz
    fb3 = jnp.concatenate([f_bias, f_bias, f_bias,
                           jnp.zeros((LANE - 3 * heads,), f_bias.dtype)])

    def row_block(t):
        return jnp.where(t == 0, n_x, t - 1)

    return pl.pallas_call(
        functools.partial(_small_proj_body, f_col % LANE, ab_col % LANE),
        grid=(n_x + 1,),
        in_specs=[pl.BlockSpec((tsz, d), lambda t: (row_block(t), 0)),
                  pl.BlockSpec((LANE, d), lambda t: (f_col // LANE, 0)),
                  pl.BlockSpec((1, LANE), lambda t: (0, 0)),
                  pl.BlockSpec((LANE, d), lambda t: (ab_col // LANE, 0))],
        out_specs=[pl.BlockSpec((tsz, LANE), lambda t: (row_block(t), 0)),
                   pl.BlockSpec((tsz, LANE), lambda t: (row_block(t), 0))],
        out_shape=[jax.ShapeDtypeStruct((lp, LANE), BF16),
                   jax.ShapeDtypeStruct((lp, LANE), F32)],
        scratch_shapes=[pltpu.VMEM((8, LANE), F32)],
        compiler_params=pltpu.CompilerParams(
            dimension_semantics=("arbitrary",), vmem_limit_bytes=VMEM_LIMIT),
        name="small_proj",
    )(n, wt, fb3.reshape(1, LANE), wt)


def _fox_body(tq, rq, hb, i_tab, j_tab, q_ref, km_ref, vm_ref, am_ref, kx_ref, vx_ref,
              ax_ref, o_ref, m_ref, acc_ref):
    g = pl.program_id(0)
    t = pl.program_id(1)
    i = i_tab[t]
    j = j_tab[t]
    dh = LANE
    lane = lax.broadcasted_iota(jnp.int32, (rq, LANE), 1)

    def q_ones(hh):
        h = g * hb + hh
        pick = jnp.logical_or(lane == h, jnp.logical_or(lane == h + FOX_HEADS,
                                                        lane == h + 2 * FOX_HEADS))
        return jnp.where(pick, 1.0, 0.0).astype(BF16)

    def online_update(k_ref, v_ref, a_ref, diagonal):
        n_k = k_ref.shape[0]
        ones = jnp.ones((n_k, LANE), BF16)
        aug = a_ref[...]
        cols_of = [slice(hh * dh, (hh + 1) * dh) for hh in range(hb)]
        keys_of = [jnp.concatenate([k_ref[:, c], aug], axis=1) for c in cols_of]
        vals_of = [jnp.concatenate([v_ref[:, c], ones], axis=1) for c in cols_of]
        pick_of = [q_ones(hh) for hh in range(hb)]
        for r in range(tq // rq):
            for hh in range(hb):
                cols, keys, vals, pick = cols_of[hh], keys_of[hh], vals_of[hh], pick_of[hh]
                rows = pl.ds(r * rq, rq)
                n_c = (r + 1) * rq if diagonal else n_k
                q = jnp.concatenate([q_ref[rows, cols], pick], axis=1)
                s = lax.dot_general(q, keys[:n_c], (((1,), (1,)), ((), ())),
                                    preferred_element_type=F32)
                if diagonal:
                    tri = (lax.broadcasted_iota(jnp.int32, (rq, rq), 1)
                           <= lax.broadcasted_iota(jnp.int32, (rq, rq), 0))
                    s_last = jnp.where(tri, s[:, n_c - rq:], NEG_BIG)
                    s = s_last if r == 0 else jnp.concatenate(
                        [s[:, :n_c - rq], s_last], axis=1)
                m_prev = m_ref[hh, rows, :]
                m_next = jnp.maximum(m_prev, jnp.max(s, axis=1, keepdims=True))
                p = jnp.exp2(s - jnp.tile(m_next, (1, n_c // LANE)))
                alpha = jnp.exp2(m_prev - m_next)
                m_ref[hh, rows, :] = m_next
                acc_ref[hh, rows, :] = (jnp.tile(alpha, (1, 2)) * acc_ref[hh, rows, :]
                                        + jnp.dot(p.astype(BF16), vals[:n_c],
                                                  preferred_element_type=F32))

    @pl.when(j == 0)
    def _():
        m_ref[...] = jnp.full_like(m_ref, NEG_BIG)
        acc_ref[...] = jnp.zeros_like(acc_ref)
        online_update(km_ref, vm_ref, am_ref, False)

    @pl.when(jnp.logical_and(j >= 1, j - 1 < i))
    def _():
        online_update(kx_ref, vx_ref, ax_ref, False)

    @pl.when(j - 1 == i)
    def _():
        online_update(kx_ref, vx_ref, ax_ref, True)
        for hh in range(hb):
            acc = acc_ref[hh]
            o_ref[:, hh * dh:(hh + 1) * dh] = (acc[:, :dh] / acc[:, dh:]).astype(
                o_ref.dtype)


def _fox_attention(qkv, kaug, *, seq, tq, rq, hb):
    heads = FOX_HEADS
    dh = FOX_HEAD_DIM
    assert dh == LANE and seq % tq == 0 and tq % rq == 0 and rq % LANE == 0
    assert heads % hb == 0
    n_q = seq // tq
    n_g = heads // hb
    meta_blk = seq // META_BLOCK
    steps = [(i, j) for i in range(n_q) for j in range(i + 2)]
    i_tab = jnp.asarray([s[0] for s in steps], jnp.int32)
    j_tab = jnp.asarray([s[1] for s in steps], jnp.int32)
    wb = hb * dh

    def kv_block(t, j_tab):
        return jnp.maximum(j_tab[t] - 1, 0)

    grid_spec = pltpu.PrefetchScalarGridSpec(
        num_scalar_prefetch=2,
        grid=(n_g, len(steps)),
        in_specs=[
            pl.BlockSpec((tq, wb), lambda g, t, it, jt: (it[t], g)),
            pl.BlockSpec((META_BLOCK, wb), lambda g, t, it, jt: (meta_blk, n_g + g)),
            pl.BlockSpec((META_BLOCK, wb),
                         lambda g, t, it, jt: (meta_blk, 2 * n_g + g)),
            pl.BlockSpec((META_BLOCK, LANE), lambda g, t, it, jt: (meta_blk, 0)),
            pl.BlockSpec((tq, wb), lambda g, t, it, jt: (kv_block(t, jt), n_g + g)),
            pl.BlockSpec((tq, wb),
                         lambda g, t, it, jt: (kv_block(t, jt), 2 * n_g + g)),
            pl.BlockSpec((tq, LANE), lambda g, t, it, jt: (kv_block(t, jt), 0)),
        ],
        out_specs=pl.BlockSpec((tq, wb), lambda g, t, it, jt: (it[t], g)),
        scratch_shapes=[pltpu.VMEM((hb, tq, LANE), F32),
                        pltpu.VMEM((hb, tq, 2 * dh), F32)],
    )
    return pl.pallas_call(
        functools.partial(_fox_body, tq, rq, hb),
        grid_spec=grid_spec,
        out_shape=jax.ShapeDtypeStruct((seq, heads * dh), BF16),
        compiler_params=pltpu.CompilerParams(
            dimension_semantics=("parallel", "arbitrary"),
            vmem_limit_bytes=VMEM_LIMIT),
        name="fox_attention",
    )(i_tab, j_tab, qkv, qkv, qkv, kaug, qkv, qkv, kaug)


def _gla_body(n_chunks, qv_ref, kr_ref, ab_ref, w2_ref, b_ref, g_ref, o_ref, state_ref):
    t = pl.program_id(0)
    is_meta = t == 0

    @pl.when(is_meta)
    def _():
        state_ref[...] = jnp.zeros_like(state_ref)

    z = jnp.dot(ab_ref[...].astype(BF16), w2_ref[...].astype(BF16),
                preferred_element_type=F32) + b_ref[...]
    la = _log_sigmoid(z) / GLA_GATE_TAU
    n_rows = la.shape[0]
    row = lax.broadcasted_iota(jnp.int32, la.shape, 0)
    la = jnp.where(jnp.logical_and(is_meta, row >= N_META), 0.0, la)
    ri = lax.broadcasted_iota(jnp.int32, (n_rows, n_rows), 0)
    ci = lax.broadcasted_iota(jnp.int32, (n_rows, n_rows), 1)
    same_chunk = (ri // CHUNK) == (ci // CHUNK)
    lower = jnp.where(jnp.logical_and(same_chunk, ci <= ri), 1.0, 0.0).astype(BF16)
    cum = _tri_matmul(lower, la)
    totals = [cum[(c + 1) * CHUNK - 1:(c + 1) * CHUNK, :] for c in range(n_chunks)]
    total_rows = jnp.concatenate(
        [jnp.broadcast_to(tot, (CHUNK, tot.shape[1])) for tot in totals], axis=0)
    k_dec = (kr_ref[:, :GLA_KW] * jnp.exp(total_rows - cum)).astype(BF16)

    for c in range(n_chunks):
        rows = slice(c * CHUNK, (c + 1) * CHUNK)
        decay = jnp.exp(totals[c])
        for h in range(GLA_HEADS):
            ks = slice(h * GLA_DK, (h + 1) * GLA_DK)
            vs = slice(GLA_KW + h * GLA_DV, GLA_KW + (h + 1) * GLA_DV)
            kv = lax.dot_general(qv_ref[rows, vs], k_dec[rows, ks],
                                 (((0,), (0,)), ((), ())),
                                 preferred_element_type=F32)
            s_new = state_ref[h] * decay[:, ks] + kv
            state_ref[h] = s_new
            o = lax.dot_general(qv_ref[rows, ks], s_new.astype(BF16),
                                (((1,), (1,)), ((), ())),
                                preferred_element_type=F32)
            ms = jnp.mean(o * o, axis=-1, keepdims=True)
            r = kr_ref[rows, vs]
            o_ref[rows, h * GLA_DV:(h + 1) * GLA_DV] = (
                o * lax.rsqrt(ms + EPS) * g_ref[...] * (r * _sigmoid(r))
            ).astype(o_ref.dtype)


def _gla(qv, kr, ab, w2, b, g, *, seq, ab_off):
    rows = CHUNK
    assert rows % CHUNK == 0 and seq % rows == 0
    n_x = seq // rows
    w2 = jnp.zeros((LANE, GLA_KW), w2.dtype).at[ab_off:ab_off + GLA_GATE_RANK].set(w2)

    def row_block(t):
        return jnp.where(t == 0, n_x, t - 1)

    width = GLA_KW + GLA_VW
    return pl.pallas_call(
        functools.partial(_gla_body, rows // CHUNK),
        grid=(n_x + 1,),
        in_specs=[pl.BlockSpec((rows, width), lambda t: (row_block(t), 0)),
                  pl.BlockSpec((rows, width), lambda t: (row_block(t), 0)),
                  pl.BlockSpec((rows, LANE), lambda t: (row_block(t), 0)),
                  pl.BlockSpec((LANE, GLA_KW), lambda t: (0, 0)),
                  pl.BlockSpec((1, GLA_KW), lambda t: (0, 0)),
                  pl.BlockSpec((1, GLA_DV), lambda t: (0, 0))],
        out_specs=pl.BlockSpec((rows, GLA_VW), lambda t: (jnp.maximum(t - 1, 0), 0)),
        out_shape=jax.ShapeDtypeStruct((seq, GLA_VW), BF16),
        scratch_shapes=[pltpu.VMEM((GLA_HEADS, GLA_DV, GLA_DK), F32)],
        compiler_params=pltpu.CompilerParams(
            dimension_semantics=("arbitrary",), vmem_limit_bytes=VMEM_LIMIT),
        name="gla",
    )(qv, kr, ab, w2, b.reshape(1, GLA_KW), g.reshape(1, GLA_DV))


def kernel(x, meta_tokens, norm1_g, ffn1_w1, ffn1_w3, ffn1_w2, norm_mix_g, w_in,
           fox_f_bias, gla_alpha_w2, gla_alpha_b, gla_norm_g, w_gate, w_proj_fox,
           w_proj_gla, w_out, norm2_g, ffn2_w1, ffn2_w3, ffn2_w2, norm_final_g):
    batch, seq, d = x.shape
    assert batch == 1 and seq % 1024 == 0
    depth = norm1_g.shape[0]
    lp = seq + META_BLOCK
    assert lp % 8 == 0 and (lp // 8) % 16 == 0
    tm_all = lp // 8
    tm_x = seq // 8

    norm_tile = 4 * META_BLOCK
    meta_pad = jnp.concatenate(
        [meta_tokens.astype(x.dtype), jnp.zeros((norm_tile - N_META, d), x.dtype)], axis=0)
    x2 = x[0]

    o_fa = FOX_WIDTH * 3
    o_qb = o_fa + FOX_HEADS
    o_kb = o_qb + GLA_KW
    o_vb = o_kb + GLA_KW
    o_rb = o_vb + GLA_VW
    o_ab = o_rb + GLA_VW
    fox_scale = FOX_HEAD_DIM ** -0.5 * LOG2E
    gla_scale = GLA_DK ** -0.5

    assert depth == 1
    for li in range(depth):
        n1 = _rmsnorm_frames_meta(x2, meta_pad, norm1_g[li], lp=lp)
        h = _swiglu_ffn(n1, x2, ffn1_w1[li], ffn1_w3[li], ffn1_w2[li], m_rows=lp,
                        tm_up=2 * tm_all, tm_down=2 * tm_all,
                        res_tail=meta_pad[:META_BLOCK])

        n = _rmsnorm(h, norm_mix_g[li], lp, BF16)
        wt = jnp.transpose(w_in[li])

        tn_p = 512
        shift = o_qb % tn_p

        def tiles(start, width):
            assert start % tn_p in (0, shift) and width % tn_p == 0
            return [start - start % tn_p + r for r in range(0, width, tn_p)]

        def fox_epilogue(dot, j):
            return dot * jnp.where(j < FOX_WIDTH // tn_p, fox_scale, 1.0)

        qkv_a = _matmul_nt(n, wt, tiles(0, 3 * FOX_WIDTH), 0, fox_epilogue, m_rows=lp,
                           out_dtype=BF16, tm=2 * tm_all, tn=tn_p, name="proj_fox")

        def gla_q_epilogue(dot, j):
            return dot * jnp.where(j < GLA_KW // tn_p, gla_scale, 1.0)

        qv_b = _matmul_nt(n, wt, tiles(o_qb, GLA_KW) + tiles(o_vb, GLA_VW), shift,
                          gla_q_epilogue, m_rows=lp, out_dtype=BF16,
                          tm=2 * tm_all, tn=tn_p, name="proj_gla_qv")
        kr_b = _matmul_nt(n, wt, tiles(o_kb, GLA_KW) + tiles(o_rb, GLA_VW), shift,
                          lambda dot, j: dot, m_rows=lp, out_dtype=F32,
                          tm=2 * tm_all, tn=tn_p, name="proj_gla_kr")

        kaug, ab = _small_proj(n, wt, fox_f_bias[li], seq=seq, f_col=o_fa, ab_col=o_ab)
        o_a = _fox_attention(qkv_a, kaug, seq=seq, tq=2048, rq=256, hb=2)
        o_b = _gla(qv_b, kr_b, ab, gla_alpha_w2[li], gla_alpha_b[li], gla_norm_g[li],
                   seq=seq, ab_off=o_ab % LANE)

        wg = w_gate[li]

        def mix_epilogue(accs, extras, j):
            ga, gb, pa, pb = accs
            return _sigmoid(ga) * pa + _sigmoid(gb) * pb

        tn_m = 256
        y = _matmul_fullk(
            [n, o_a, o_b],
            [(0, wg, 0), (0, wg, d // tn_m), (1, w_proj_fox[li], 0),
             (2, w_proj_gla[li], 0)],
            [], mix_epilogue, m_rows=seq, n_cols=d, out_dtype=BF16,
            tm=tm_x, tn=tn_m, name="mix", a_buffers=[1, 2, 2])

        h = _matmul_fullk([y], [(0, w_out[li], 0)], [h],
                          lambda a, e, j: e[0] + a[0], m_rows=seq, n_cols=d,
                          out_dtype=F32, tm=2 * tm_x, tn=512, name="out_proj", a_buffers=1)

        n2 = _rmsnorm(h, norm2_g[li], seq, BF16)
        h = _swiglu_ffn(n2, h, ffn2_w1[li], ffn2_w3[li], ffn2_w2[li], m_rows=seq,
                        tm_up=2 * tm_x, tm_down=2 * tm_x)

    out = _rmsnorm(h, norm_final_g, seq, F32)
    return out.reshape(batch, seq, d)
```

```python
import functools

import jax
import jax.numpy as jnp
from jax import lax
from jax.experimental import pallas as pl
from jax.experimental.pallas import tpu as pltpu

F32 = jnp.float32
BF16 = jnp.bfloat16

EPS = 1e-6
N_META = 16
CHUNK = 64
FOX_HEADS = 16
FOX_HEAD_DIM = 128
FOX_WIDTH = FOX_HEADS * FOX_HEAD_DIM
GLA_HEADS = 4
GLA_DK = 256
GLA_DV = 512
GLA_KW = GLA_HEADS * GLA_DK
GLA_VW = GLA_HEADS * GLA_DV
GLA_GATE_RANK = 16
GLA_GATE_TAU = 16.0

LANE = 128
META_BLOCK = LANE
NEG_BIG = -1e30
LOG2E = 1.4426950408889634
VMEM_LIMIT = 60 * 1024 * 1024


def _log_sigmoid(z):
    return jnp.minimum(z, 0.0) - jnp.log(1.0 + jnp.exp(-jnp.abs(z)))


def _sigmoid(z):
    return 1.0 / (1.0 + jnp.exp(-z))


def _tri_matmul(tri, x):
    hi = x.astype(BF16)
    rest = x - hi.astype(F32)
    mid = rest.astype(BF16)
    lo = (rest - mid.astype(F32)).astype(BF16)
    return (jnp.dot(tri, hi, preferred_element_type=F32)
            + jnp.dot(tri, mid, preferred_element_type=F32)
            + jnp.dot(tri, lo, preferred_element_type=F32))


def _rmsnorm_body(x_ref, g_ref, o_ref):
    x = x_ref[...]
    ms = jnp.mean(x * x, axis=-1, keepdims=True)
    o_ref[...] = (x * lax.rsqrt(ms + EPS) * g_ref[...]).astype(o_ref.dtype)


def _rmsnorm(x, g, rows, out_dtype):
    d = x.shape[1]
    tile = max(t for t in range(16, 513, 16) if rows % t == 0)
    return pl.pallas_call(
        _rmsnorm_body,
        grid=(rows // tile,),
        in_specs=[pl.BlockSpec((tile, d), lambda i: (i, 0)),
                  pl.BlockSpec((1, d), lambda i: (0, 0))],
        out_specs=pl.BlockSpec((tile, d), lambda i: (i, 0)),
        out_shape=jax.ShapeDtypeStruct((rows, d), out_dtype),
        compiler_params=pltpu.CompilerParams(
            dimension_semantics=("parallel",), vmem_limit_bytes=VMEM_LIMIT),
        name="rmsnorm",
    )(x, g.reshape(1, d))


def _rmsnorm_frames_meta_body(n_x, x_ref, mp_ref, g_ref, o_ref):
    src = jnp.where(pl.program_id(0) < n_x, x_ref[...], mp_ref[...])
    ms = jnp.mean(src * src, axis=-1, keepdims=True)
    o_ref[...] = (src * lax.rsqrt(ms + EPS) * g_ref[...]).astype(o_ref.dtype)


def _rmsnorm_frames_meta(x, meta_pad, g, *, lp):
    seq, d = x.shape
    tile = meta_pad.shape[0]
    assert seq % tile == 0 and seq < lp <= seq + tile
    n_x = seq // tile
    return pl.pallas_call(
        functools.partial(_rmsnorm_frames_meta_body, n_x),
        grid=(n_x + 1,),
        in_specs=[pl.BlockSpec((tile, d), lambda i: (jnp.minimum(i, n_x - 1), 0)),
                  pl.BlockSpec((tile, d), lambda i: (0, 0)),
                  pl.BlockSpec((1, d), lambda i: (0, 0))],
        out_specs=pl.BlockSpec((tile, d), lambda i: (i, 0)),
        out_shape=jax.ShapeDtypeStruct((lp, d), BF16),
        compiler_params=pltpu.CompilerParams(
            dimension_semantics=("parallel",), vmem_limit_bytes=VMEM_LIMIT),
        name="rmsnorm_in",
    )(x, meta_pad, g.reshape(1, d))


def _mm_fullk_body(pair_ai, n_a, n_extra, epilogue, *refs):
    n_p = len(pair_ai)
    a_refs = refs[:n_a]
    b_refs = refs[n_a:n_a + n_p]
    e_refs = refs[n_a + n_p:n_a + n_p + n_extra]
    o_ref = refs[n_a + n_p + n_extra]
    dots = [jnp.dot(a_refs[ai][...], b_refs[p][...].astype(BF16),
                    preferred_element_type=F32) for p, ai in enumerate(pair_ai)]
    extras = [r[...] for r in e_refs]
    o_ref[...] = epilogue(dots, extras, pl.program_id(1)).astype(o_ref.dtype)


def _matmul_fullk(a_list, pairs, extras, epilogue, *, m_rows, n_cols, out_dtype,
                  tm, tn, name, a_buffers=2):
    assert m_rows % tm == 0 and n_cols % tn == 0
    if isinstance(a_buffers, int):
        a_buffers = [a_buffers] * len(a_list)
    in_specs = [pl.BlockSpec((tm, a.shape[1]), lambda i, j: (i, 0),
                             **({} if nb == 2 else {"pipeline_mode": pl.Buffered(nb)}))
                for a, nb in zip(a_list, a_buffers)]
    for ai, b, col_off in pairs:
        assert b.shape[0] == a_list[ai].shape[1]
        in_specs.append(pl.BlockSpec((b.shape[0], tn),
                                     lambda i, j, col_off=col_off: (0, j + col_off)))
    for _ in extras:
        in_specs.append(pl.BlockSpec((tm, tn), lambda i, j: (i, j)))
    body = functools.partial(_mm_fullk_body, tuple(ai for ai, _, _ in pairs),
                             len(a_list), len(extras), epilogue)
    return pl.pallas_call(
        body,
        grid=(m_rows // tm, n_cols // tn),
        in_specs=in_specs,
        out_specs=pl.BlockSpec((tm, tn), lambda i, j: (i, j)),
        out_shape=jax.ShapeDtypeStruct((m_rows, n_cols), out_dtype),
        compiler_params=pltpu.CompilerParams(
            dimension_semantics=("parallel", "arbitrary"),
            vmem_limit_bytes=VMEM_LIMIT),
        name=name,
    )(*a_list, *[b for _, b, _ in pairs], *extras)


def _mm_nt_body(shift, epilogue, tab_ref, a_ref, bm_ref, *rest):
    del tab_ref
    b = bm_ref[...]
    if shift:
        bn_ref, o_ref = rest
        b = jnp.concatenate([b[shift:], bn_ref[...]], axis=0)
    else:
        (o_ref,) = rest
    d = lax.dot_general(a_ref[...], b.astype(BF16), (((1,), (1,)), ((), ())),
                        preferred_element_type=F32)
    o_ref[...] = epilogue(d, pl.program_id(1)).astype(o_ref.dtype)


def _matmul_nt(a, bt, tile_starts, shift, epilogue, *, m_rows, out_dtype, tm, tn, name):
    assert m_rows % tm == 0 and all(s % tn == 0 for s in tile_starts)
    assert shift % 8 == 0 and (shift == 0 or tn % shift == 0)
    k_total = a.shape[1]
    assert bt.shape[1] == k_total
    tab = jnp.asarray([s // tn for s in tile_starts], jnp.int32)
    in_specs = [pl.BlockSpec((tm, k_total), lambda i, j, tab: (i, 0),
                             pipeline_mode=pl.Buffered(1)),
                pl.BlockSpec((tn, k_total), lambda i, j, tab: (tab[j], 0))]
    operands = [tab, a, bt]
    if shift:
        per = tn // shift
        in_specs.append(pl.BlockSpec((shift, k_total),
                                     lambda i, j, tab: ((tab[j] + 1) * per, 0)))
        operands.append(bt)
    grid_spec = pltpu.PrefetchScalarGridSpec(
        num_scalar_prefetch=1,
        grid=(m_rows // tm, len(tile_starts)),
        in_specs=in_specs,
        out_specs=pl.BlockSpec((tm, tn), lambda i, j, tab: (i, j)),
    )
    return pl.pallas_call(
        functools.partial(_mm_nt_body, shift, epilogue),
        grid_spec=grid_spec,
        out_shape=jax.ShapeDtypeStruct((m_rows, tn * len(tile_starts)), out_dtype),
        compiler_params=pltpu.CompilerParams(
            dimension_semantics=("parallel", "arbitrary"),
            vmem_limit_bytes=VMEM_LIMIT),
        name=name,
    )(*operands)


def _mm_kacc_body(tk, k_total, scale, has_tail, a_ref, b_ref, r_ref, *rest):
    if has_tail:
        tail_ref, o_ref = rest
    else:
        (o_ref,) = rest
    k = pl.program_id(2)
    last_row_block = pl.program_id(0) == pl.num_programs(0) - 1

    def residual():
        r = r_ref[...]
        if has_tail:
            tm, t_rows = r.shape[0], tail_ref.shape[0]
            tail = jnp.concatenate(
                [jnp.zeros((tm - t_rows, r.shape[1]), r.dtype), tail_ref[...]], axis=0)
            row = lax.broadcasted_iota(jnp.int32, r.shape, 0)
            in_tail = jnp.logical_and(last_row_block, row >= tm - t_rows)
            r = jnp.where(in_tail, tail, r)
        return r

    def partial_product():
        a = a_ref[...]
        b = b_ref[...]
        if k_total % tk:
            limit = k_total - k * tk
            a = jnp.where(lax.broadcasted_iota(jnp.int32, a.shape, 1) < limit, a,
                          jnp.zeros_like(a))
            b = jnp.where(lax.broadcasted_iota(jnp.int32, b.shape, 0) < limit, b,
                          jnp.zeros_like(b))
        return scale * jnp.dot(a, b.astype(BF16), preferred_element_type=F32)

    @pl.when(k == 0)
    def _():
        o_ref[...] = residual() + partial_product()

    @pl.when(k > 0)
    def _():
        o_ref[...] += partial_product()


def _matmul_kacc(a, b, res, scale, *, m_rows, tm, tn, tk, name, res_tail=None):
    k_total, n_cols = b.shape
    assert a.shape[1] == k_total and m_rows % tm == 0 and n_cols % tn == 0
    in_specs = [pl.BlockSpec((tm, tk), lambda i, j, k: (i, k)),
                pl.BlockSpec((tk, tn), lambda i, j, k: (k, j)),
                pl.BlockSpec((tm, tn), lambda i, j, k: (i, j))]
    operands = [a, b, res]
    if res_tail is not None:
        assert res.shape[0] + res_tail.shape[0] == m_rows
        in_specs.append(pl.BlockSpec((res_tail.shape[0], tn), lambda i, j, k: (0, j)))
        operands.append(res_tail)
    return pl.pallas_call(
        functools.partial(_mm_kacc_body, tk, k_total, scale, res_tail is not None),
        grid=(m_rows // tm, n_cols // tn, pl.cdiv(k_total, tk)),
        in_specs=in_specs,
        out_specs=pl.BlockSpec((tm, tn), lambda i, j, k: (i, j)),
        out_shape=jax.ShapeDtypeStruct((m_rows, n_cols), F32),
        compiler_params=pltpu.CompilerParams(
            dimension_semantics=("parallel", "parallel", "arbitrary"),
            vmem_limit_bytes=VMEM_LIMIT),
        name=name,
    )(*operands)


def _swiglu_ffn(n, h_res, w1, w3, w2, *, m_rows, tm_up, tm_down, res_tail=None):
    def up_epilogue(dots, extras, j):
        a1, a3 = dots
        return a1 * _sigmoid(a1) * a3

    u = _matmul_fullk([n], [(0, w1, 0), (0, w3, 0)], [], up_epilogue,
                      m_rows=m_rows, n_cols=w1.shape[1], out_dtype=BF16,
                      tm=tm_up, tn=256, name="ffn_up")
    return _matmul_kacc(u, w2, h_res, 0.5, m_rows=m_rows, tm=tm_down, tn=1024, tk=1024,
                        name="ffn_down", res_tail=res_tail)


def _small_proj_body(f_off, ab_off, n_ref, wf_ref, fb_ref, wab_ref, kaug_ref, ab_ref,
                     carry_ref):
    t = pl.program_id(0)
    is_meta = t == 0
    nt_dims = (((1,), (1,)), ((), ()))

    @pl.when(is_meta)
    def _():
        carry_ref[...] = jnp.zeros_like(carry_ref)

    n = n_ref[...]
    wrow = lax.broadcasted_iota(jnp.int32, wab_ref.shape, 0)
    wab = jnp.where(jnp.logical_and(wrow >= ab_off, wrow < ab_off + GLA_GATE_RANK),
                    wab_ref[...], 0.0)
    both = lax.dot_general(n, jnp.concatenate([wf_ref[...], wab], axis=0).astype(BF16),
                           nt_dims, preferred_element_type=F32)
    ft = both[:, :LANE]
    ab_ref[...] = both[:, LANE:]
    row = lax.broadcasted_iota(jnp.int32, ft.shape, 0)
    col = lax.broadcasted_iota(jnp.int32, ft.shape, 1)
    ft = jnp.where(jnp.logical_and(col >= f_off, col < f_off + FOX_HEADS), ft, 0.0)
    shifts = [(p * FOX_HEADS - f_off) % LANE for p in range(3)]
    ft = sum(pltpu.roll(ft, s, 1) if s else ft for s in shifts)
    logf = _log_sigmoid(ft + fb_ref[...])
    valid = jnp.logical_or(jnp.logical_not(is_meta), row < N_META)
    logf = jnp.where(valid, logf, 0.0)
    tsz = logf.shape[0]
    lower = jnp.where(lax.broadcasted_iota(jnp.int32, (tsz, tsz), 1)
                      <= lax.broadcasted_iota(jnp.int32, (tsz, tsz), 0),
                      1.0, 0.0).astype(BF16)
    c = carry_ref[0:1, :] + _tri_matmul(lower, logf)
    carry_ref[...] = carry_ref[...] + jnp.sum(logf, axis=0, keepdims=True)

    bias = -LOG2E * c
    hi = bias.astype(BF16).astype(F32)
    mid = (bias - hi).astype(BF16).astype(F32)
    lo = (bias - hi - mid).astype(BF16).astype(F32)
    pieces = jnp.where(col < FOX_HEADS, hi,
                       jnp.where(col < 2 * FOX_HEADS, mid,
                                 jnp.where(col < 3 * FOX_HEADS, lo, 0.0)))
    removed = jnp.where(col < FOX_HEADS, NEG_BIG, 0.0)
    kaug_ref[...] = jnp.where(valid, pieces, removed).astype(BF16)


def _small_proj(n, wt, f_bias, *, seq, f_col, ab_col):
    lp, d = n.shape
    heads = f_bias.shape[0]
    assert heads == FOX_HEADS and 3 * heads <= LANE
    assert f_col % LANE + heads <= LANE and ab_col % LANE + GLA_GATE_RANK <= LANE
    tsz = 4 * META_BLOCK
    assert seq % tsz == 0
    n_x = seq // tsz
    fb3 = jnp.concatenate([f_bias, f_bias, f_bias,
                           jnp.zeros((LANE - 3 * heads,), f_bias.dtype)])

    def row_block(t):
        return jnp.where(t == 0, n_x, t - 1)

    return pl.pallas_call(
        functools.partial(_small_proj_body, f_col % LANE, ab_col % LANE),
        grid=(n_x + 1,),
        in_specs=[pl.BlockSpec((tsz, d), lambda t: (row_block(t), 0)),
                  pl.BlockSpec((LANE, d), lambda t: (f_col // LANE, 0)),
                  pl.BlockSpec((1, LANE), lambda t: (0, 0)),
                  pl.BlockSpec((LANE, d), lambda t: (ab_col // LANE, 0))],
        out_specs=[pl.BlockSpec((tsz, LANE), lambda t: (row_block(t), 0)),
                   pl.BlockSpec((tsz, LANE), lambda t: (row_block(t), 0))],
        out_shape=[jax.ShapeDtypeStruct((lp, LANE), BF16),
                   jax.ShapeDtypeStruct((lp, LANE), F32)],
        scratch_shapes=[pltpu.VMEM((8, LANE), F32)],
        compiler_params=pltpu.CompilerParams(
            dimension_semantics=("arbitrary",), vmem_limit_bytes=VMEM_LIMIT),
        name="small_proj",
    )(n, wt, fb3.reshape(1, LANE), wt)


def _fox_body(tq, rq, hb, i_tab, j_tab, q_ref, km_ref, vm_ref, am_ref, kx_ref, vx_ref,
              ax_ref, o_ref, m_ref, acc_ref):
    g = pl.program_id(0)
    t = pl.program_id(1)
    i = i_tab[t]
    j = j_tab[t]
    dh = LANE
    lane = lax.broadcasted_iota(jnp.int32, (rq, LANE), 1)

    def q_ones(hh):
        h = g * hb + hh
        pick = jnp.logical_or(lane == h, jnp.logical_or(lane == h + FOX_HEADS,
                                                        lane == h + 2 * FOX_HEADS))
        return jnp.where(pick, 1.0, 0.0).astype(BF16)

    def online_update(k_ref, v_ref, a_ref, diagonal):
        n_k = k_ref.shape[0]
        ones = jnp.ones((n_k, LANE), BF16)
        aug = a_ref[...]
        cols_of = [slice(hh * dh, (hh + 1) * dh) for hh in range(hb)]
        keys_of = [jnp.concatenate([k_ref[:, c], aug], axis=1) for c in cols_of]
        vals_of = [jnp.concatenate([v_ref[:, c], ones], axis=1) for c in cols_of]
        pick_of = [q_ones(hh) for hh in range(hb)]
        for r in range(tq // rq):
            for hh in range(hb):
                cols, keys, vals, pick = cols_of[hh], keys_of[hh], vals_of[hh], pick_of[hh]
                rows = pl.ds(r * rq, rq)
                n_c = (r + 1) * rq if diagonal else n_k
                q = jnp.concatenate([q_ref[rows, cols], pick], axis=1)
                s = lax.dot_general(q, keys[:n_c], (((1,), (1,)), ((), ())),
                                    preferred_element_type=F32)
                if diagonal:
                    tri = (lax.broadcasted_iota(jnp.int32, (rq, rq), 1)
                           <= lax.broadcasted_iota(jnp.int32, (rq, rq), 0))
                    s_last = jnp.where(tri, s[:, n_c - rq:], NEG_BIG)
                    s = s_last if r == 0 else jnp.concatenate(
                        [s[:, :n_c - rq], s_last], axis=1)
                m_prev = m_ref[hh, rows, :]
                m_next = jnp.maximum(m_prev, jnp.max(s, axis=1, keepdims=True))
                p = jnp.exp2(s - jnp.tile(m_next, (1, n_c // LANE)))
                alpha = jnp.exp2(m_prev - m_next)
                m_ref[hh, rows, :] = m_next
                acc_ref[hh, rows, :] = (jnp.tile(alpha, (1, 2)) * acc_ref[hh, rows, :]
                                        + jnp.dot(p.astype(BF16), vals[:n_c],
                                                  preferred_element_type=F32))

    @pl.when(j == 0)
    def _():
        m_ref[...] = jnp.full_like(m_ref, NEG_BIG)
        acc_ref[...] = jnp.zeros_like(acc_ref)
        online_update(km_ref, vm_ref, am_ref, False)

    @pl.when(jnp.logical_and(j >= 1, j - 1 < i))
    def _():
        online_update(kx_ref, vx_ref, ax_ref, False)

    @pl.when(j - 1 == i)
    def _():
        online_update(kx_ref, vx_ref, ax_ref, True)
        for hh in range(hb):
            acc = acc_ref[hh]
            o_ref[:, hh * dh:(hh + 1) * dh] = (acc[:, :dh] / acc[:, dh:]).astype(
                o_ref.dtype)


def _fox_attention(qkv, kaug, *, seq, tq, rq, hb):
    heads = FOX_HEADS
    dh = FOX_HEAD_DIM
    assert dh == LANE and seq % tq == 0 and tq % rq == 0 and rq % LANE == 0
    assert heads % hb == 0
    n_q = seq // tq
    n_g = heads // hb
    meta_blk = seq // META_BLOCK
    steps = [(i, j) for i in range(n_q) for j in range(i + 2)]
    i_tab = jnp.asarray([s[0] for s in steps], jnp.int32)
    j_tab = jnp.asarray([s[1] for s in steps], jnp.int32)
    wb = hb * dh

    def kv_block(t, j_tab):
        return jnp.maximum(j_tab[t] - 1, 0)

    grid_spec = pltpu.PrefetchScalarGridSpec(
        num_scalar_prefetch=2,
        grid=(n_g, len(steps)),
        in_specs=[
            pl.BlockSpec((tq, wb), lambda g, t, it, jt: (it[t], g)),
            pl.BlockSpec((META_BLOCK, wb), lambda g, t, it, jt: (meta_blk, n_g + g)),
            pl.BlockSpec((META_BLOCK, wb),
                         lambda g, t, it, jt: (meta_blk, 2 * n_g + g)),
            pl.BlockSpec((META_BLOCK, LANE), lambda g, t, it, jt: (meta_blk, 0)),
            pl.BlockSpec((tq, wb), lambda g, t, it, jt: (kv_block(t, jt), n_g + g)),
            pl.BlockSpec((tq, wb),
                         lambda g, t, it, jt: (kv_block(t, jt), 2 * n_g + g)),
            pl.BlockSpec((tq, LANE), lambda g, t, it, jt: (kv_block(t, jt), 0)),
        ],
        out_specs=pl.BlockSpec((tq, wb), lambda g, t, it, jt: (it[t], g)),
        scratch_shapes=[pltpu.VMEM((hb, tq, LANE), F32),
                        pltpu.VMEM((hb, tq, 2 * dh), F32)],
    )
    return pl.pallas_call(
        functools.partial(_fox_body, tq, rq, hb),
        grid_spec=grid_spec,
        out_shape=jax.ShapeDtypeStruct((seq, heads * dh), BF16),
        compiler_params=pltpu.CompilerParams(
            dimension_semantics=("parallel", "arbitrary"),
            vmem_limit_bytes=VMEM_LIMIT),
        name="fox_attention",
    )(i_tab, j_tab, qkv, qkv, qkv, kaug, qkv, qkv, kaug)


def _gla_body(n_chunks, qv_ref, kr_ref, ab_ref, w2_ref, b_ref, g_ref, o_ref, state_ref):
    t = pl.program_id(0)
    is_meta = t == 0

    @pl.when(is_meta)
    def _():
        state_ref[...] = jnp.zeros_like(state_ref)

    z = jnp.dot(ab_ref[...].astype(BF16), w2_ref[...].astype(BF16),
                preferred_element_type=F32) + b_ref[...]
    la = _log_sigmoid(z) / GLA_GATE_TAU
    n_rows = la.shape[0]
    row = lax.broadcasted_iota(jnp.int32, la.shape, 0)
    la = jnp.where(jnp.logical_and(is_meta, row >= N_META), 0.0, la)
    ri = lax.broadcasted_iota(jnp.int32, (n_rows, n_rows), 0)
    ci = lax.broadcasted_iota(jnp.int32, (n_rows, n_rows), 1)
    same_chunk = (ri // CHUNK) == (ci // CHUNK)
    lower = jnp.where(jnp.logical_and(same_chunk, ci <= ri), 1.0, 0.0).astype(BF16)
    cum = _tri_matmul(lower, la)
    totals = [cum[(c + 1) * CHUNK - 1:(c + 1) * CHUNK, :] for c in range(n_chunks)]
    total_rows = jnp.concatenate(
        [jnp.broadcast_to(tot, (CHUNK, tot.shape[1])) for tot in totals], axis=0)
    k_dec = (kr_ref[:, :GLA_KW] * jnp.exp(total_rows - cum)).astype(BF16)

    for c in range(n_chunks):
        rows = slice(c * CHUNK, (c + 1) * CHUNK)
        decay = jnp.exp(totals[c])
        for h in range(GLA_HEADS):
            ks = slice(h * GLA_DK, (h + 1) * GLA_DK)
            vs = slice(GLA_KW + h * GLA_DV, GLA_KW + (h + 1) * GLA_DV)
            kv = lax.dot_general(qv_ref[rows, vs], k_dec[rows, ks],
                                 (((0,), (0,)), ((), ())),
                                 preferred_element_type=F32)
            s_new = state_ref[h] * decay[:, ks] + kv
            state_ref[h] = s_new
            o = lax.dot_general(qv_ref[rows, ks], s_new.astype(BF16),
                                (((1,), (1,)), ((), ())),
                                preferred_element_type=F32)
            ms = jnp.mean(o * o, axis=-1, keepdims=True)
            r = kr_ref[rows, vs]
            o_ref[rows, h * GLA_DV:(h + 1) * GLA_DV] = (
                o * lax.rsqrt(ms + EPS) * g_ref[...] * (r * _sigmoid(r))
            ).astype(o_ref.dtype)


def _gla(qv, kr, ab, w2, b, g, *, seq, ab_off):
    rows = CHUNK
    assert rows % CHUNK == 0 and seq % rows == 0
    n_x = seq // rows
    w2 = jnp.zeros((LANE, GLA_KW), w2.dtype).at[ab_off:ab_off + GLA_GATE_RANK].set(w2)

    def row_block(t):
        return jnp.where(t == 0, n_x, t - 1)

    width = GLA_KW + GLA_VW
    return pl.pallas_call(
        functools.partial(_gla_body, rows // CHUNK),
        grid=(n_x + 1,),
        in_specs=[pl.BlockSpec((rows, width), lambda t: (row_block(t), 0)),
                  pl.BlockSpec((rows, width), lambda t: (row_block(t), 0)),
                  pl.BlockSpec((rows, LANE), lambda t: (row_block(t), 0)),
                  pl.BlockSpec((LANE, GLA_KW), lambda t: (0, 0)),
                  pl.BlockSpec((1, GLA_KW), lambda t: (0, 0)),
                  pl.BlockSpec((1, GLA_DV), lambda t: (0, 0))],
        out_specs=pl.BlockSpec((rows, GLA_VW), lambda t: (jnp.maximum(t - 1, 0), 0)),
        out_shape=jax.ShapeDtypeStruct((seq, GLA_VW), BF16),
        scratch_shapes=[pltpu.VMEM((GLA_HEADS, GLA_DV, GLA_DK), F32)],
        compiler_params=pltpu.CompilerParams(
            dimension_semantics=("arbitrary",), vmem_limit_bytes=VMEM_LIMIT),
        name="gla",
    )(qv, kr, ab, w2, b.reshape(1, GLA_KW), g.reshape(1, GLA_DV))


def kernel(x, meta_tokens, norm1_g, ffn1_w1, ffn1_w3, ffn1_w2, norm_mix_g, w_in,
           fox_f_bias, gla_alpha_w2, gla_alpha_b, gla_norm_g, w_gate, w_proj_fox,
           w_proj_gla, w_out, norm2_g, ffn2_w1, ffn2_w3, ffn2_w2, norm_final_g):
    batch, seq, d = x.shape
    assert batch == 1 and seq % 1024 == 0
    depth = norm1_g.shape[0]
    lp = seq + META_BLOCK
    assert lp % 8 == 0 and (lp // 8) % 16 == 0
    tm_all = lp // 8
    tm_x = seq // 8

    norm_tile = 4 * META_BLOCK
    meta_pad = jnp.concatenate(
        [meta_tokens.astype(x.dtype), jnp.zeros((norm_tile - N_META, d), x.dtype)], axis=0)
    x2 = x[0]

    o_fa = FOX_WIDTH * 3
    o_qb = o_fa + FOX_HEADS
    o_kb = o_qb + GLA_KW
    o_vb = o_kb + GLA_KW
    o_rb = o_vb + GLA_VW
    o_ab = o_rb + GLA_VW
    fox_scale = FOX_HEAD_DIM ** -0.5 * LOG2E
    gla_scale = GLA_DK ** -0.5

    assert depth == 1
    for li in range(depth):
        n1 = _rmsnorm_frames_meta(x2, meta_pad, norm1_g[li], lp=lp)
        h = _swiglu_ffn(n1, x2, ffn1_w1[li], ffn1_w3[li], ffn1_w2[li], m_rows=lp,
                        tm_up=2 * tm_all, tm_down=2 * tm_all,
                        res_tail=meta_pad[:META_BLOCK])

        n = _rmsnorm(h, norm_mix_g[li], lp, BF16)
        wt = jnp.transpose(w_in[li])

        tn_p = 512
        shift = o_qb % tn_p

        def tiles(start, width):
            assert start % tn_p in (0, shift) and width % tn_p == 0
            return [start - start % tn_p + r for r in range(0, width, tn_p)]

        def fox_epilogue(dot, j):
            return dot * jnp.where(j < FOX_WIDTH // tn_p, fox_scale, 1.0)

        qkv_a = _matmul_nt(n, wt, tiles(0, 3 * FOX_WIDTH), 0, fox_epilogue, m_rows=lp,
                           out_dtype=BF16, tm=2 * tm_all, tn=tn_p, name="proj_fox")

        def gla_q_epilogue(dot, j):
            return dot * jnp.where(j < GLA_KW // tn_p, gla_scale, 1.0)

        qv_b = _matmul_nt(n, wt, tiles(o_qb, GLA_KW) + tiles(o_vb, GLA_VW), shift,
                          gla_q_epilogue, m_rows=lp, out_dtype=BF16,
                          tm=2 * tm_all, tn=tn_p, name="proj_gla_qv")
        kr_b = _matmul_nt(n, wt, tiles(o_kb, GLA_KW) + tiles(o_rb, GLA_VW), shift,
                          lambda dot, j: dot, m_rows=lp, out_dtype=F32,
                          tm=2 * tm_all, tn=tn_p, name="proj_gla_kr")

        kaug, ab = _small_proj(n, wt, fox_f_bias[li], seq=seq, f_col=o_fa, ab_col=o_ab)
        o_a = _fox_attention(qkv_a, kaug, seq=seq, tq=2048, rq=256, hb=2)
        o_b = _gla(qv_b, kr_b, ab, gla_alpha_w2[li], gla_alpha_b[li], gla_norm_g[li],
                   seq=seq, ab_off=o_ab % LANE)

        wg = w_gate[li]

        def mix_epilogue(accs, extras, j):
            ga, gb, pa, pb = accs
            return _sigmoid(ga) * pa + _sigmoid(gb) * pb

        tn_m = 256
        y = _matmul_fullk(
            [n, o_a, o_b],
            [(0, wg, 0), (0, wg, d // tn_m), (1, w_proj_fox[li], 0),
             (2, w_proj_gla[li], 0)],
            [], mix_epilogue, m_rows=seq, n_cols=d, out_dtype=BF16,
            tm=tm_x, tn=tn_m, name="mix", a_buffers=[1, 2, 2])

        h = _matmul_fullk([y], [(0, w_out[li], 0)], [h],
                          lambda a, e, j: e[0] + a[0], m_rows=seq, n_cols=d,
                          out_dtype=F32, tm=2 * tm_x, tn=512, name="out_proj", a_buffers=1)

        n2 = _rmsnorm(h, norm2_g[li], seq, BF16)
        h = _swiglu_ffn(n2, h, ffn2_w1[li], ffn2_w3[li], ffn2_w2[li], m_rows=seq,
                        tm_up=2 * tm_x, tm_down=2 * tm_x)

    out = _rmsnorm(h, norm_final_g, seq, F32)
    return out.reshape(batch, seq, d)
```
